```python
import jax, jax.numpy as jnp
from jax import lax
import numpy as np

D_MODEL = 1024
BATCH = 1
SEQ = 16384
DEPTH = 2

N_Q_HEADS = 8
N_KV_HEADS = 2
HEAD_DIM = 64
Q_PER_KV = N_Q_HEADS // N_KV_HEADS
ATTN_WIDTH = N_Q_HEADS * HEAD_DIM
KV_WIDTH = N_KV_HEADS * HEAD_DIM
WINDOW = 128
ROT_DIM = HEAD_DIM // 4
ROPE_THETA = 500000.0
CONF_WIDTH = D_MODEL // 2
CONF_KERNEL = 31
E_IN_COLS = ATTN_WIDTH + 2 * KV_WIDTH + 2 * CONF_WIDTH
E_OUT_COLS = ATTN_WIDTH + CONF_WIDTH
SSM_EXPAND = 2
D_INNER = SSM_EXPAND * D_MODEL
SSM_HEADDIM = 64
SSM_HEADS = D_INNER // SSM_HEADDIM
SSM_GROUPS = 4
HEADS_PER_GROUP = SSM_HEADS // SSM_GROUPS
D_STATE = 128
SSM_CONV = 4
CHUNK = 128
XBC_WIDTH = D_INNER + 2 * SSM_GROUPS * D_STATE
O_IN_COLS = D_INNER + XBC_WIDTH + SSM_HEADS
D_FF = 2816
FFN_CONV = 3
N_EVEN = (DEPTH + 1) // 2
N_ODD = DEPTH // 2
NORM_EPS = 1e-6
LN_EPS = 1e-5

kernel_name = 'hybrid_swa_conformer_ssd_convffn_adaln'


def rmsnorm(x, g):
    xf = x.astype(jnp.float32)
    y = xf * lax.rsqrt(jnp.mean(xf * xf, axis=-1, keepdims=True) + NORM_EPS)
    return (y * g.astype(jnp.float32)).astype(x.dtype)


def causal_dwconv(u, w, b):
    k = w.shape[0]
    ch = u.shape[-1]
    out = lax.conv_general_dilated(u, w.astype(u.dtype)[:, None, :], (1,), ((k - 1, 0),),
                                   dimension_numbers=('NWC', 'WIO', 'NWC'),
                                   feature_group_count=ch)
    return out + b.astype(u.dtype)


def rope_partial(x, positions):
    half = ROT_DIM // 2
    inv_freq = 1.0 / (ROPE_THETA ** (jnp.arange(half, dtype=jnp.float32) * 2.0 / ROT_DIM))
    ang = positions.astype(jnp.float32)[..., None] * inv_freq
    cos = jnp.cos(ang)[:, :, None, :]
    sin = jnp.sin(ang)[:, :, None, :]
    xf = x.astype(jnp.float32)
    x1 = xf[..., :half]
    x2 = xf[..., half:ROT_DIM]
    return jnp.concatenate([x1 * cos - x2 * sin, x2 * cos + x1 * sin, xf[..., ROT_DIM:]], axis=-1)


def band_blocks(t):
    b, s = t.shape[:2]
    nb = s // WINDOW
    cur = t.reshape(b, nb, WINDOW, t.shape[2], t.shape[3])
    prev = jnp.pad(cur, ((0, 0), (1, 0), (0, 0), (0, 0), (0, 0)))[:, :-1]
    return jnp.concatenate([prev, cur], axis=2)


def swa_sink_attention(q, k, v, sinks, positions):
    b, s = q.shape[:2]
    nb = s // WINDOW
    qb = rope_partial(q, positions).reshape(b, nb, WINDOW, N_KV_HEADS, Q_PER_KV, HEAD_DIM)
    kb = band_blocks(rope_partial(k, positions))
    vb = band_blocks(v.astype(jnp.float32))
    scores = jnp.einsum('bnqhgd,bnkhd->bnhgqk', qb, kb) * (HEAD_DIM ** -0.5)
    qi = jnp.arange(WINDOW)[:, None]
    kj = jnp.arange(2 * WINDOW)[None, :]
    band = (kj > qi) & (kj <= qi + WINDOW)
    key_pos = jnp.arange(nb)[:, None, None] * WINDOW + kj[None] - WINDOW
    mask = band[None] & (key_pos >= 0)
    scores = jnp.where(mask[None, :, None, None], scores, -jnp.inf)
    sink = sinks.astype(jnp.float32).reshape(1, 1, N_KV_HEADS, Q_PER_KV, 1, 1)
    m = jnp.maximum(scores.max(axis=-1, keepdims=True), sink)
    p = jnp.exp(scores - m)
    denom = p.sum(axis=-1, keepdims=True) + jnp.exp(sink - m)
    out = jnp.einsum('bnhgqk,bnkhd->bnqhgd', p / denom, vb)
    return out.reshape(b, s, ATTN_WIDTH)


def conformer_conv(u, conv_w, conv_b, ln_g, ln_b):
    a, g = jnp.split(u, 2, axis=-1)
    h = causal_dwconv(a * jax.nn.sigmoid(g), conv_w, conv_b).astype(jnp.float32)
    mu = jnp.mean(h, axis=-1, keepdims=True)
    var = jnp.mean(jnp.square(h - mu), axis=-1, keepdims=True)
    h = (h - mu) * lax.rsqrt(var + LN_EPS) * ln_g.astype(jnp.float32) + ln_b.astype(jnp.float32)
    return jax.nn.silu(h)


def even_mixer(h, positions, w_in, b_in, sinks, conv_w, conv_b, ln_g, ln_b, w_out):
    b, s = h.shape[:2]
    proj = h @ w_in + b_in
    q, k, v, conf_in = jnp.split(proj, [ATTN_WIDTH, ATTN_WIDTH + KV_WIDTH,
                                        ATTN_WIDTH + 2 * KV_WIDTH], axis=-1)
    q = q.reshape(b, s, N_Q_HEADS, HEAD_DIM)
    k = k.reshape(b, s, N_KV_HEADS, HEAD_DIM)
    v = v.reshape(b, s, N_KV_HEADS, HEAD_DIM)
    attn = swa_sink_attention(q, k, v, sinks, positions)
    conf = conformer_conv(conf_in, conv_w, conv_b, ln_g, ln_b)
    merged = jnp.concatenate([attn, conf], axis=-1).astype(h.dtype)
    return merged @ w_out


def ssd_chunked(xdt, a, bm, cm):
    b, s = xdt.shape[:2]
    nc = s // CHUNK
    xc = xdt.reshape(b, nc, CHUNK, SSM_GROUPS, HEADS_PER_GROUP, SSM_HEADDIM)
    ac = a.reshape(b, nc, CHUNK, SSM_GROUPS, HEADS_PER_GROUP)
    bc = bm.reshape(b, nc, CHUNK, SSM_GROUPS, D_STATE)
    cc = cm.reshape(b, nc, CHUNK, SSM_GROUPS, D_STATE)
    a_cs = jnp.cumsum(ac, axis=2)
    causal = jnp.tril(jnp.ones((CHUNK, CHUNK), dtype=bool))
    seg = a_cs[:, :, :, None] - a_cs[:, :, None, :]
    decay_in = jnp.exp(jnp.where(causal[:, :, None, None], seg, -jnp.inf))
    cb = jnp.einsum('bclgn,bcsgn->bclsg', cc, bc)
    y_diag = jnp.einsum('bclsgj,bcsgjp->bclgjp', cb[..., None] * decay_in, xc)
    decay_to_end = jnp.exp(a_cs[:, :, -1:] - a_cs)
    states = jnp.einsum('bcsgn,bcsgjp->bcgjpn', bc, xc * decay_to_end[..., None])
    chunk_decay = jnp.exp(a_cs[:, :, -1])

    def step(state, inp):
        dec, st = inp
        return dec[..., None, None] * state + st, state

    h0 = jnp.zeros((b, SSM_GROUPS, HEADS_PER_GROUP, SSM_HEADDIM, D_STATE), jnp.float32)
    _, prev = lax.scan(step, h0, (jnp.moveaxis(chunk_decay, 1, 0), jnp.moveaxis(states, 1, 0)))
    prev = jnp.moveaxis(prev, 0, 1)
    y_off = jnp.einsum('bclgn,bcgjpn->bclgjp', cc, prev) * jnp.exp(a_cs)[..., None]
    return (y_diag + y_off).reshape(b, s, SSM_GROUPS, HEADS_PER_GROUP, SSM_HEADDIM)


def ssd_mixer(h, w_in, conv_w, conv_b, dt_bias, a_log, d_skip, norm_g, w_out):
    b, s = h.shape[:2]
    proj = h @ w_in
    z, xbc, dt = jnp.split(proj, [D_INNER, D_INNER + XBC_WIDTH], axis=-1)
    xbc = jax.nn.silu(causal_dwconv(xbc, conv_w, conv_b))
    xs, bm, cm = jnp.split(xbc, [D_INNER, D_INNER + SSM_GROUPS * D_STATE], axis=-1)
    xs = xs.astype(jnp.float32).reshape(b, s, SSM_GROUPS, HEADS_PER_GROUP, SSM_HEADDIM)
    bm = bm.astype(jnp.float32).reshape(b, s, SSM_GROUPS, D_STATE)
    cm = cm.astype(jnp.float32).reshape(b, s, SSM_GROUPS, D_STATE)
    dt = jax.nn.softplus(dt.astype(jnp.float32) + dt_bias.astype(jnp.float32))
    dt = dt.reshape(b, s, SSM_GROUPS, HEADS_PER_GROUP)
    a = -jnp.exp(a_log.astype(jnp.float32)).reshape(SSM_GROUPS, HEADS_PER_GROUP)
    y = ssd_chunked(xs * dt[..., None], dt * a, bm, cm)
    y = y + xs * d_skip.astype(jnp.float32).reshape(SSM_GROUPS, HEADS_PER_GROUP, 1)
    y = y.reshape(b, s, D_INNER) * jax.nn.silu(z.astype(jnp.float32))
    yg = y.reshape(b, s, SSM_GROUPS, D_INNER // SSM_GROUPS)
    yg = yg * lax.rsqrt(jnp.mean(yg * yg, axis=-1, keepdims=True) + NORM_EPS)
    y = yg.reshape(b, s, D_INNER) * norm_g.astype(jnp.float32)
    return y.astype(h.dtype) @ w_out


def conv_ffn(h, w_gate, conv_w, conv_b, w_val, w_down):
    g = causal_dwconv(h @ w_gate, conv_w, conv_b)
    return (jax.nn.silu(g) * (h @ w_val)) @ w_down


def setup_inputs(seed: int = 0) -> dict:
    key = jax.random.key(seed)
    ks = iter(jax.random.split(key, 40))

    def nrm(shape, scale):
        return jax.random.normal(next(ks), shape, jnp.float32) * scale

    def gain(shape):
        return 1.0 + nrm(shape, 0.02)

    x = nrm((BATCH, SEQ, D_MODEL), 1.0)
    c = nrm((BATCH, D_MODEL), 1.0)
    positions = jnp.broadcast_to(jnp.arange(SEQ, dtype=jnp.int32)[None, :], (BATCH, SEQ))
    w_mod = nrm((DEPTH, D_MODEL, 6 * D_MODEL), 0.5 * D_MODEL ** -0.5)
    b_mod = nrm((DEPTH, 6 * D_MODEL), 0.01)
    norm_mix = gain((DEPTH, D_MODEL))
    norm_ffn = gain((DEPTH, D_MODEL))
    w_in_e = nrm((N_EVEN, D_MODEL, E_IN_COLS), D_MODEL ** -0.5)
    b_in_e = nrm((N_EVEN, E_IN_COLS), 0.01)
    attn_sinks = nrm((N_EVEN, N_Q_HEADS), 1.0)
    conf_conv_w = nrm((N_EVEN, CONF_KERNEL, CONF_WIDTH), CONF_KERNEL ** -0.5)
    conf_conv_b = nrm((N_EVEN, CONF_WIDTH), 0.01)
    conf_ln_g = gain((N_EVEN, CONF_WIDTH))
    conf_ln_b = nrm((N_EVEN, CONF_WIDTH), 0.01)
    w_out_e = nrm((N_EVEN, E_OUT_COLS, D_MODEL), E_OUT_COLS ** -0.5)
    w_in_o = nrm((N_ODD, D_MODEL, O_IN_COLS), D_MODEL ** -0.5)
    ssm_conv_w = nrm((N_ODD, SSM_CONV, XBC_WIDTH), SSM_CONV ** -0.5)
    ssm_conv_b = nrm((N_ODD, XBC_WIDTH), 0.01)
    dt0 = jnp.exp(jax.random.uniform(next(ks), (N_ODD, SSM_HEADS), jnp.float32,
                                     np.log(1e-3).astype(np.float32), np.log(1e-1).astype(np.float32)))
    ssm_dt_bias = dt0 + jnp.log(-jnp.expm1(-dt0))
    ssm_a_log = jnp.log(jax.random.uniform(next(ks), (N_ODD, SSM_HEADS), jnp.float32, 1.0, 16.0))
    ssm_d = 1.0 + nrm((N_ODD, SSM_HEADS), 0.1)
    ssm_norm_g = gain((N_ODD, D_INNER))
    w_out_o = nrm((N_ODD, D_INNER, D_MODEL), D_INNER ** -0.5)
    ffn_w_gate = nrm((DEPTH, D_MODEL, D_FF), D_MODEL ** -0.5)
    ffn_conv_w = nrm((DEPTH, FFN_CONV, D_FF), FFN_CONV ** -0.5)
    ffn_conv_b = nrm((DEPTH, D_FF), 0.01)
    ffn_w_val = nrm((DEPTH, D_MODEL, D_FF), D_MODEL ** -0.5)
    ffn_w_down = nrm((DEPTH, D_FF, D_MODEL), D_FF ** -0.5)
    final_norm = gain((D_MODEL,))
    return {'x': x, 'c': c, 'positions': positions, 'w_mod': w_mod, 'b_mod': b_mod,
            'norm_mix': norm_mix, 'norm_ffn': norm_ffn, 'w_in_e': w_in_e, 'b_in_e': b_in_e,
            'attn_sinks': attn_sinks, 'conf_conv_w': conf_conv_w, 'conf_conv_b': conf_conv_b,
            'conf_ln_g': conf_ln_g, 'conf_ln_b': conf_ln_b, 'w_out_e': w_out_e,
            'w_in_o': w_in_o, 'ssm_conv_w': ssm_conv_w, 'ssm_conv_b': ssm_conv_b,
            'ssm_dt_bias': ssm_dt_bias, 'ssm_a_log': ssm_a_log, 'ssm_d': ssm_d,
            'ssm_norm_g': ssm_norm_g, 'w_out_o': w_out_o, 'ffn_w_gate': ffn_w_gate,
            'ffn_conv_w': ffn_conv_w, 'ffn_conv_b': ffn_conv_b, 'ffn_w_val': ffn_w_val,
            'ffn_w_down': ffn_w_down, 'final_norm': final_norm}


def reference(x, c, positions, w_mod, b_mod, norm_mix, norm_ffn, w_in_e, b_in_e, attn_sinks,
              conf_conv_w, conf_conv_b, conf_ln_g, conf_ln_b, w_out_e, w_in_o, ssm_conv_w,
              ssm_conv_b, ssm_dt_bias, ssm_a_log, ssm_d, ssm_norm_g, w_out_o, ffn_w_gate,
              ffn_conv_w, ffn_conv_b, ffn_w_val, ffn_w_down, final_norm):
    silu_c = jax.nn.silu(c)
    for i in range(DEPTH):
        mod = (silu_c @ w_mod[i] + b_mod[i])[:, None, :]
        sh_m, sc_m, g_m, sh_f, sc_f, g_f = jnp.split(mod, 6, axis=-1)
        h = rmsnorm(x, norm_mix[i]) * (1.0 + sc_m) + sh_m
        j = i // 2
        if i % 2 == 0:
            y = even_mixer(h, positions, w_in_e[j], b_in_e[j], attn_sinks[j], conf_conv_w[j],
                           conf_conv_b[j], conf_ln_g[j], conf_ln_b[j], w_out_e[j])
        else:
            y = ssd_mixer(h, w_in_o[j], ssm_conv_w[j], ssm_conv_b[j], ssm_dt_bias[j],
                          ssm_a_log[j], ssm_d[j], ssm_norm_g[j], w_out_o[j])
        x = x + g_m * y
        h = rmsnorm(x, norm_ffn[i]) * (1.0 + sc_f) + sh_f
        x = x + g_f * conv_ffn(h, ffn_w_gate[i], ffn_conv_w[i], ffn_conv_b[i],
                               ffn_w_val[i], ffn_w_down[i])
    return rmsnorm(x, final_norm)
```

```python
import functools

import jax
import jax.numpy as jnp
from jax import lax
from jax.experimental import pallas as pl
from jax.experimental.pallas import tpu as pltpu

F32 = jnp.float32
BF16 = jnp.bfloat16

D_MODEL = 1024
N_Q_HEADS = 8
N_KV_HEADS = 2
HEAD_DIM = 64
ATTN_WIDTH = N_Q_HEADS * HEAD_DIM
KV_WIDTH = N_KV_HEADS * HEAD_DIM
WINDOW = 128
ROT_DIM = HEAD_DIM // 4
ROPE_THETA = 500000.0
CONF_WIDTH = D_MODEL // 2
CONF_KERNEL = 31
E_IN_COLS = ATTN_WIDTH + 2 * KV_WIDTH + 2 * CONF_WIDTH
D_INNER = 2 * D_MODEL
SSM_HEADDIM = 64
SSM_HEADS = D_INNER // SSM_HEADDIM
SSM_GROUPS = 4
HEADS_PER_GROUP = SSM_HEADS // SSM_GROUPS
D_STATE = 128
SSM_CONV = 4
CHUNK = 128
XBC_WIDTH = D_INNER + 2 * SSM_GROUPS * D_STATE
D_FF = 2816
FFN_CONV = 3
NORM_EPS = 1e-6
LN_EPS = 1e-5

LANES = 128
SUBLANES = 8
VMEM_LIMIT_BYTES = 56 * 1024 * 1024

TM_MIX0 = 512
TM_FFN = 256
TM_SSD = 256
CONF_ROWS = 32
CONF_HALO = 32
CONV_HALO = 8


def _const_spec(shape):
    nd = len(shape)
    return pl.BlockSpec(shape, lambda i: (0,) * nd, pipeline_mode=pl.Buffered(1))


def _row_spec(tm, width):
    return pl.BlockSpec((tm, width), lambda i: (i, 0))


def _sigmoid(v):
    return 1.0 / (1.0 + jnp.exp(-v))


def _silu(v):
    return v * _sigmoid(v)


def _mod_norm(x, gain, scale, shift):
    ms = jnp.mean(x * x, axis=-1, keepdims=True)
    return (x * lax.rsqrt(ms + NORM_EPS)) * (gain * (1.0 + scale)) + shift


def _dot(a, b):
    return jnp.dot(a, b, preferred_element_type=F32)


def _mod_kernel(c_ref, w_ref, b_ref, o_ref):
    c = c_ref[...]
    s = _silu(c)
    o_ref[0] = jnp.sum(w_ref[0] * s, axis=0, keepdims=True) + b_ref[0]


def _modulation(c, w_mod, b_mod):
    depth, d, n = w_mod.shape
    nb = 1536
    assert n % nb == 0
    return pl.pallas_call(
        _mod_kernel,
        grid=(depth, n // nb),
        in_specs=[
            pl.BlockSpec((d, 1), lambda i, j: (0, 0)),
            pl.BlockSpec((1, d, nb), lambda i, j: (i, 0, j)),
            pl.BlockSpec((1, 1, nb), lambda i, j: (i, 0, j)),
        ],
        out_specs=pl.BlockSpec((1, 1, nb), lambda i, j: (i, 0, j)),
        out_shape=jax.ShapeDtypeStruct((depth, 1, n), F32),
        compiler_params=pltpu.CompilerParams(
            dimension_semantics=("arbitrary", "arbitrary"), vmem_limit_bytes=VMEM_LIMIT_BYTES),
        name="mod",
    )(c.reshape(d, 1), w_mod, b_mod.reshape(depth, 1, n))


def _mix0_kernel(x_ref, pos_ref, mod_ref, nm_ref, freq_ref, win_ref, bin_ref, sink_ref, cw_ref, cb_ref,
                 lng_ref, lnb_ref, wout_ref, o_ref, kbuf, vbuf, uext, merged, *, tm):
    i = pl.program_id(0)

    @pl.when(i == 0)
    def _():
        kbuf[:, 0:WINDOW, :] = jnp.zeros((4, WINDOW, LANES), BF16)
        vbuf[:, 0:WINDOW, :] = jnp.zeros((4, WINDOW, LANES), BF16)
        uext[0:CONF_HALO, :] = jnp.zeros((CONF_HALO, CONF_WIDTH), F32)

    x = x_ref[...]
    shift = mod_ref[:, 0:D_MODEL]
    scale = mod_ref[:, D_MODEL:2 * D_MODEL]
    gate = mod_ref[:, 2 * D_MODEL:3 * D_MODEL]
    h = _mod_norm(x, nm_ref[...], scale, shift).astype(BF16)
    proj = _dot(h, win_ref[...]) + bin_ref[...]

    ang = pos_ref[...].astype(F32) * freq_ref[...]
    cosv = jnp.cos(ang)
    sinv = jnp.sin(ang)
    lane = lax.broadcasted_iota(jnp.int32, (1, LANES), 1)
    hl = lane & (HEAD_DIM - 1)
    sin_up = jnp.where((hl >= ROT_DIM // 2) & (hl < ROT_DIM), sinv, 0.0)
    sin_dn = jnp.where(hl < ROT_DIM // 2, -sinv, 0.0)
    lo = lane < HEAD_DIM

    def rope(t):
        return (t * cosv + pltpu.roll(t, ROT_DIM // 2, 1) * sin_up
                + pltpu.roll(t, LANES - ROT_DIM // 2, 1) * sin_dn)

    k = rope(proj[:, ATTN_WIDTH:ATTN_WIDTH + KV_WIDTH])
    v = proj[:, ATTN_WIDTH + KV_WIDTH:ATTN_WIDTH + 2 * KV_WIDTH]
    for buf, t in ((kbuf, k), (vbuf, v)):
        t_sw = pltpu.roll(t, HEAD_DIM, 1)
        buf[0, WINDOW:WINDOW + tm, :] = jnp.where(lo, t, 0.0).astype(BF16)
        buf[1, WINDOW:WINDOW + tm, :] = jnp.where(lo, 0.0, t_sw).astype(BF16)
        buf[2, WINDOW:WINDOW + tm, :] = jnp.where(lo, t_sw, 0.0).astype(BF16)
        buf[3, WINDOW:WINDOW + tm, :] = jnp.where(lo, 0.0, t).astype(BF16)

    q_slabs = [rope(proj[:, s * LANES:(s + 1) * LANES] * (HEAD_DIM ** -0.5)).astype(BF16)
               for s in range(ATTN_WIDTH // LANES)]

    row = lax.broadcasted_iota(jnp.int32, (2 * WINDOW, 1), 0)
    qi = row & (WINDOW - 1)
    kj = lax.broadcasted_iota(jnp.int32, (1, 2 * WINDOW), 1)
    band = (kj > qi) & (kj <= qi + WINDOW)
    band_first = band & ((kj >= WINDOW) | (i > 0))

    for b in range(tm // WINDOW):
        r0 = b * WINDOW
        mask = band_first if b == 0 else band
        for kh in range(N_KV_HEADS):
            qs = jnp.concatenate([q_slabs[2 * kh][r0:r0 + WINDOW], q_slabs[2 * kh + 1][r0:r0 + WINDOW]], axis=0)
            acc = None
            for par in range(2):
                sink = jnp.where(row < WINDOW, sink_ref[4 * kh + par], sink_ref[4 * kh + 2 + par])
                kk = kbuf[2 * kh + par, r0:r0 + 2 * WINDOW, :]
                s = lax.dot_general(qs, kk, (((1,), (1,)), ((), ())), preferred_element_type=F32)
                s = jnp.where(mask, s, -jnp.inf)
                m = jnp.maximum(jnp.max(s, axis=-1, keepdims=True), sink)
                p = jnp.exp(s - m)
                denom = jnp.sum(p, axis=-1, keepdims=True) + jnp.exp(sink - m)
                o = _dot(p.astype(BF16), vbuf[2 * kh + par, r0:r0 + 2 * WINDOW, :]) * (1.0 / denom)
                acc = o if acc is None else acc + o
            merged[r0:r0 + WINDOW, 2 * kh * LANES:(2 * kh + 1) * LANES] = acc[0:WINDOW].astype(BF16)
            merged[r0:r0 + WINDOW, (2 * kh + 1) * LANES:(2 * kh + 2) * LANES] = acc[WINDOW:].astype(BF16)

    kbuf[:, 0:WINDOW, :] = kbuf[:, tm:tm + WINDOW, :]
    vbuf[:, 0:WINDOW, :] = vbuf[:, tm:tm + WINDOW, :]

    c0 = ATTN_WIDTH + 2 * KV_WIDTH
    uext[CONF_HALO:CONF_HALO + tm, :] = proj[:, c0:c0 + CONF_WIDTH] * _sigmoid(proj[:, c0 + CONF_WIDTH:])
    tap0 = CONF_HALO - (CONF_KERNEL - 1)

    for r0 in range(0, tm, CONF_ROWS):
        acc = jnp.zeros((CONF_ROWS, CONF_WIDTH), F32) + cb_ref[...]
        for kk in range(CONF_KERNEL):
            acc = acc + cw_ref[kk:kk + 1, :] * uext[r0 + tap0 + kk:r0 + tap0 + kk + CONF_ROWS, :]
        mu = jnp.mean(acc, axis=-1, keepdims=True)
        cen = acc - mu
        var = jnp.mean(cen * cen, axis=-1, keepdims=True)
        hn = cen * lax.rsqrt(var + LN_EPS) * lng_ref[...] + lnb_ref[...]
        merged[r0:r0 + CONF_ROWS, ATTN_WIDTH:] = _silu(hn).astype(BF16)
    uext[0:CONF_HALO, :] = uext[tm:tm + CONF_HALO, :]

    y = _dot(merged[...], wout_ref[...])
    o_ref[...] = x + gate * y


def _mix0(x, pos, mod, norm_g, freq, w_in, b_in, sinks, conv_w, conv_b, ln_g, ln_b, w_out):
    s = x.shape[0]
    tm = TM_MIX0
    assert s % tm == 0 and tm % WINDOW == 0 and tm % CONF_ROWS == 0
    return pl.pallas_call(
        functools.partial(_mix0_kernel, tm=tm),
        grid=(s // tm,),
        in_specs=[
            _row_spec(tm, D_MODEL),
            _row_spec(tm, 1),
            _const_spec(mod.shape),
            _const_spec(norm_g.shape),
            _const_spec(freq.shape),
            _const_spec(w_in.shape),
            _const_spec(b_in.shape),
            pl.BlockSpec(memory_space=pltpu.SMEM),
            _const_spec(conv_w.shape),
            _const_spec(conv_b.shape),
            _const_spec(ln_g.shape),
            _const_spec(ln_b.shape),
            _const_spec(w_out.shape),
        ],
        out_specs=_row_spec(tm, D_MODEL),
        out_shape=jax.ShapeDtypeStruct((s, D_MODEL), F32),
        scratch_shapes=[
            pltpu.VMEM((4, WINDOW + tm, LANES), BF16),
            pltpu.VMEM((4, WINDOW + tm, LANES), BF16),
            pltpu.VMEM((CONF_HALO + tm, CONF_WIDTH), F32),
            pltpu.VMEM((tm, D_MODEL), BF16),
        ],
        compiler_params=pltpu.CompilerParams(
            dimension_semantics=("arbitrary",), vmem_limit_bytes=VMEM_LIMIT_BYTES),
        name="mix0",
    )(x, pos, mod, norm_g, freq, w_in, b_in, sinks, conv_w, conv_b, ln_g, ln_b, w_out)


def _ffn_kernel(x_ref, mod_ref, nf_ref, wg_ref, wv_ref, cw_ref, cb_ref, wd_ref, fn_ref, o_ref, gext,
                *, tm, final_norm):
    i = pl.program_id(0)

    @pl.when(i == 0)
    def _():
        gext[0:CONV_HALO, :] = jnp.zeros((CONV_HALO, D_FF), F32)

    x = x_ref[...]
    shift = mod_ref[:, 3 * D_MODEL:4 * D_MODEL]
    scale = mod_ref[:, 4 * D_MODEL:5 * D_MODEL]
    gate = mod_ref[:, 5 * D_MODEL:6 * D_MODEL]
    h = _mod_norm(x, nf_ref[...], scale, shift).astype(BF16)
    gext[CONV_HALO:CONV_HALO + tm, :] = _dot(h, wg_ref[...])
    val = _dot(h, wv_ref[...])
    g = cb_ref[...]
    for kk in range(FFN_CONV):
        r = CONV_HALO - (FFN_CONV - 1) + kk
        g = g + cw_ref[kk:kk + 1, :] * gext[r:r + tm, :]
    gext[0:CONV_HALO, :] = gext[tm:tm + CONV_HALO, :]
    act = (_silu(g) * val).astype(BF16)
    y = x + gate * _dot(act, wd_ref[...])
    if final_norm:
        ms = jnp.mean(y * y, axis=-1, keepdims=True)
        y = y * lax.rsqrt(ms + NORM_EPS) * fn_ref[...]
    o_ref[...] = y


def _ffn(x, mod, norm_g, w_gate, conv_w, conv_b, w_val, w_down, final_g, final_norm):
    s = x.shape[0]
    tm = TM_FFN
    assert s % tm == 0
    return pl.pallas_call(
        functools.partial(_ffn_kernel, tm=tm, final_norm=final_norm),
        grid=(s // tm,),
        in_specs=[
            _row_spec(tm, D_MODEL),
            _const_spec(mod.shape),
            _const_spec(norm_g.shape),
            _const_spec(w_gate.shape),
            _const_spec(w_val.shape),
            _const_spec(conv_w.shape),
            _const_spec(conv_b.shape),
            _const_spec(w_down.shape),
            _const_spec(final_g.shape),
        ],
        out_specs=_row_spec(tm, D_MODEL),
        out_shape=jax.ShapeDtypeStruct((s, D_MODEL), F32),
        scratch_shapes=[pltpu.VMEM((CONV_HALO + tm, D_FF), F32)],
        compiler_params=pltpu.CompilerParams(
            dimension_semantics=("arbitrary",), vmem_limit_bytes=VMEM_LIMIT_BYTES),
        name="ffn_final" if final_norm else "ffn",
    )(x, mod, norm_g, w_gate, w_val, conv_w, conv_b, w_down, final_g)


def _split3(v):
    hi = v.astype(BF16)
    r1 = v - hi.astype(F32)
    mid = r1.astype(BF16)
    lo = (r1 - mid.astype(F32)).astype(BF16)
    return hi, mid, lo


def _ssd_kernel(x_ref, mod_ref, nm_ref, wz_ref, wx_ref, wdt_ref, cw_ref, cb_ref, dtb_ref, alog_ref, dskip_ref,
                ng_ref, wo_ref, o_ref, xext, xact, dts, adts, state, yscr, *, tm):
    i = pl.program_id(0)

    @pl.when(i == 0)
    def _():
        xext[0:CONV_HALO, :] = jnp.zeros((CONV_HALO, XBC_WIDTH), F32)
        state[...] = jnp.zeros((D_STATE, D_INNER), F32)

    x = x_ref[...]
    shift = mod_ref[:, 0:D_MODEL]
    scale = mod_ref[:, D_MODEL:2 * D_MODEL]
    gate = mod_ref[:, 2 * D_MODEL:3 * D_MODEL]
    h = _mod_norm(x, nm_ref[...], scale, shift).astype(BF16)

    xext[CONV_HALO:CONV_HALO + tm, :] = _dot(h, wx_ref[...])
    xbc = cb_ref[...]
    for kk in range(SSM_CONV):
        r = CONV_HALO - (SSM_CONV - 1) + kk
        xbc = xbc + cw_ref[kk:kk + 1, :] * xext[r:r + tm, :]
    xext[0:CONV_HALO, :] = xext[tm:tm + CONV_HALO, :]
    xact[...] = _silu(xbc)

    dt_in = _dot(h, wdt_ref[...]) + dtb_ref[...]
    dt = jnp.maximum(dt_in, 0.0) + jnp.log1p(jnp.exp(-jnp.abs(dt_in)))
    dts[...] = dt
    adts[...] = dt * (-jnp.exp(alog_ref[...]))

    ri = lax.broadcasted_iota(jnp.int32, (CHUNK, CHUNK), 0)
    ci = lax.broadcasted_iota(jnp.int32, (CHUNK, CHUNK), 1)
    causal = ri >= ci
    tril = jnp.where(causal, 1.0, 0.0).astype(BF16)
    lo = lax.broadcasted_iota(jnp.int32, (1, LANES), 1) < SSM_HEADDIM
    b0 = D_INNER
    c0 = D_INNER + SSM_GROUPS * D_STATE

    def chunk_body(c, carry):
        r = pl.multiple_of(c * CHUNK, CHUNK)
        dt_c = dts[pl.ds(r, CHUNK), :]
        hi, mid, low = _split3(adts[pl.ds(r, CHUNK), :])
        a_cs = _dot(tril, hi) + _dot(tril, mid) + _dot(tril, low)
        a_cs_t = a_cs.T
        dt_t = dt_c.T
        a_last = a_cs[CHUNK - 1:CHUNK, :]
        w_end = dt_c * jnp.exp(a_last - a_cs)
        cdec = jnp.exp(a_last)
        for g in range(SSM_GROUPS):
            bm_g = xact[pl.ds(r, CHUNK), b0 + g * D_STATE:b0 + (g + 1) * D_STATE]
            cm_g = xact[pl.ds(r, CHUNK), c0 + g * D_STATE:c0 + (g + 1) * D_STATE]
            cb = lax.dot_general(cm_g.astype(BF16), bm_g.astype(BF16), (((1,), (1,)), ((), ())),
                                 preferred_element_type=F32)
            for jp in range(HEADS_PER_GROUP // 2):
                sl = g * (HEADS_PER_GROUP // 2) + jp
                xs_p = xact[pl.ds(r, CHUNK), sl * LANES:(sl + 1) * LANES]
                st_p = state[:, sl * LANES:(sl + 1) * LANES]
                y_p = None
                s_new = None
                for par in range(2):
                    hd = 2 * sl + par
                    keep = lo if par == 0 else jnp.logical_not(lo)
                    xs_m = jnp.where(keep, xs_p, 0.0).astype(BF16)
                    st_m = jnp.where(keep, st_p, 0.0).astype(BF16)
                    a_l = a_cs[:, hd:hd + 1]
                    a_s = a_cs_t[hd:hd + 1, :]
                    decay = jnp.exp(jnp.where(causal, a_l - a_s, -jnp.inf))
                    m_in = cb * decay * dt_t[hd:hd + 1, :]
                    c_off = cm_g * jnp.exp(a_l)
                    lhs = jnp.concatenate([m_in, c_off], axis=1).astype(BF16)
                    rhs = jnp.concatenate([xs_m, st_m], axis=0)
                    y_h = _dot(lhs, rhs)
                    b_dec = (bm_g * w_end[:, hd:hd + 1]).astype(BF16)
                    s_h = lax.dot_general(b_dec, xs_m, (((0,), (0,)), ((), ())), preferred_element_type=F32)
                    y_p = y_h if y_p is None else y_p + y_h
                    s_new = s_h if s_new is None else s_new + s_h
                cd_p = jnp.where(lo, cdec[:, 2 * sl:2 * sl + 1], cdec[:, 2 * sl + 1:2 * sl + 2])
                state[:, sl * LANES:(sl + 1) * LANES] = cd_p * st_p + s_new
                yscr[pl.ds(r, CHUNK), sl * LANES:(sl + 1) * LANES] = (
                    y_p + xs_p * dskip_ref[:, sl * LANES:(sl + 1) * LANES])
        return carry

    lax.fori_loop(0, tm // CHUNK, chunk_body, 0)

    z = _dot(h, wz_ref[...])
    y = yscr[...] * _silu(z)
    gw = D_INNER // SSM_GROUPS
    parts = []
    for g in range(SSM_GROUPS):
        yg = y[:, g * gw:(g + 1) * gw]
        ms = jnp.mean(yg * yg, axis=-1, keepdims=True)
        parts.append((yg * lax.rsqrt(ms + NORM_EPS) * ng_ref[:, g * gw:(g + 1) * gw]).astype(BF16))
    yn = jnp.concatenate(parts, axis=1)
    o_ref[...] = x + gate * _dot(yn, wo_ref[...])


def _ssd(x, mod, norm_g, w_z, w_x, w_dt, conv_w, conv_b, dt_bias, a_log, d_skip, norm_y, w_out):
    s = x.shape[0]
    tm = TM_SSD
    assert s % tm == 0 and tm % CHUNK == 0
    return pl.pallas_call(
        functools.partial(_ssd_kernel, tm=tm),
        grid=(s // tm,),
        in_specs=[
            _row_spec(tm, D_MODEL),
            _const_spec(mod.shape),
            _const_spec(norm_g.shape),
            _const_spec(w_z.shape),
            _const_spec(w_x.shape),
            _const_spec(w_dt.shape),
            _const_spec(conv_w.shape),
            _const_spec(conv_b.shape),
            _const_spec(dt_bias.shape),
            _const_spec(a_log.shape),
            _const_spec(d_skip.shape),
            _const_spec(norm_y.shape),
            _const_spec(w_out.shape),
        ],
        out_specs=_row_spec(tm, D_MODEL),
        out_shape=jax.ShapeDtypeStruct((s, D_MODEL), F32),
        scratch_shapes=[
            pltpu.VMEM((CONV_HALO + tm, XBC_WIDTH), F32),
            pltpu.VMEM((tm, XBC_WIDTH), F32),
            pltpu.VMEM((tm, LANES), F32),
            pltpu.VMEM((tm, LANES), F32),
            pltpu.VMEM((D_STATE, D_INNER), F32),
            pltpu.VMEM((tm, D_INNER), F32),
        ],
        compiler_params=pltpu.CompilerParams(
            dimension_semantics=("arbitrary",), vmem_limit_bytes=VMEM_LIMIT_BYTES),
        name="ssd",
    )(x, mod, norm_g, w_z, w_x, w_dt, conv_w, conv_b, dt_bias, a_log, d_skip, norm_y, w_out)


def _pad_lanes(v, width=LANES):
    return jnp.pad(v, ((0, 0), (0, width - v.shape[1])))


def kernel(x, c, positions, w_mod, b_mod, norm_mix, norm_ffn, w_in_e, b_in_e, attn_sinks, conf_conv_w,
           conf_conv_b, conf_ln_g, conf_ln_b, w_out_e, w_in_o, ssm_conv_w, ssm_conv_b, ssm_dt_bias, ssm_a_log,
           ssm_d, ssm_norm_g, w_out_o, ffn_w_gate, ffn_conv_w, ffn_conv_b, ffn_w_val, ffn_w_down, final_norm):
    b, s, d = x.shape
    assert b == 1 and d == D_MODEL and w_mod.shape[0] == 2
    xs = x.reshape(s, d)
    pos = positions.reshape(s, 1)

    mod = _modulation(c, w_mod, b_mod)

    half = ROT_DIM // 2
    inv_freq = 1.0 / (ROPE_THETA ** (jnp.arange(half, dtype=F32) * 2.0 / ROT_DIM))
    head_freq = jnp.concatenate([inv_freq, inv_freq, jnp.zeros((HEAD_DIM - ROT_DIM,), F32)])
    freq = jnp.tile(head_freq, LANES // HEAD_DIM).reshape(1, LANES)

    row = lambda v: v.reshape(1, -1)
    final_g = row(final_norm)

    xs = _mix0(xs, pos, mod[0], row(norm_mix[0]), freq, w_in_e[0].astype(BF16), row(b_in_e[0]), attn_sinks[0],
               jnp.pad(conf_conv_w[0], ((0, 1), (0, 0))), row(conf_conv_b[0]), row(conf_ln_g[0]),
               row(conf_ln_b[0]), w_out_e[0].astype(BF16))
    xs = _ffn(xs, mod[0], row(norm_ffn[0]), ffn_w_gate[0].astype(BF16), ffn_conv_w[0], row(ffn_conv_b[0]),
              ffn_w_val[0].astype(BF16), ffn_w_down[0].astype(BF16), final_g, False)

    w_in = w_in_o[0]
    w_z = w_in[:, :D_INNER].astype(BF16)
    w_x = w_in[:, D_INNER:D_INNER + XBC_WIDTH].astype(BF16)
    w_dt = _pad_lanes(w_in[:, D_INNER + XBC_WIDTH:]).astype(BF16)
    d_skip = jnp.repeat(ssm_d[0], SSM_HEADDIM).reshape(1, D_INNER)
    xs = _ssd(xs, mod[1], row(norm_mix[1]), w_z, w_x, w_dt, ssm_conv_w[0], row(ssm_conv_b[0]),
              _pad_lanes(row(ssm_dt_bias[0])), _pad_lanes(row(ssm_a_log[0])), d_skip, row(ssm_norm_g[0]),
              w_out_o[0].astype(BF16))
    xs = _ffn(xs, mod[1], row(norm_ffn[1]), ffn_w_gate[1].astype(BF16), ffn_conv_w[1], row(ffn_conv_b[1]),
              ffn_w_val[1].astype(BF16), ffn_w_down[1].astype(BF16), final_g, True)
    return xs.reshape(b, s, d)
```

```python
import functools

import jax
import jax.numpy as jnp
from jax import lax
from jax.experimental import pallas as pl
from jax.experimental.pallas import tpu as pltpu

F32 = jnp.float32
BF16 = jnp.bfloat16

D_MODEL = 1024
N_Q_HEADS = 8
N_KV_HEADS = 2
HEAD_DIM = 64
ATTN_WIDTH = N_Q_HEADS * HEAD_DIM
KV_WIDTH = N_KV_HEADS * HEAD_DIM
WINDOW = 128
ROT_DIM = HEAD_DIM // 4
ROPE_THETA = 500000.0
CONF_WIDTH = D_MODEL // 2
CONF_KERNEL = 31
E_IN_COLS = ATTN_WIDTH + 2 * KV_WIDTH + 2 * CONF_WIDTH
D_INNER = 2 * D_MODEL
SSM_HEADDIM = 64
SSM_HEADS = D_INNER // SSM_HEADDIM
SSM_GROUPS = 4
HEADS_PER_GROUP = SSM_HEADS // SSM_GROUPS
D_STATE = 128
SSM_CONV = 4
CHUNK = 128
XBC_WIDTH = D_INNER + 2 * SSM_GROUPS * D_STATE
D_FF = 2816
FFN_CONV = 3
NORM_EPS = 1e-6
LN_EPS = 1e-5

LANES = 128
SUBLANES = 8
VMEM_LIMIT_BYTES = 56 * 1024 * 1024

TM_MIX0 = 512
TM_FFN = 256
TM_SSD = 256
CONF_ROWS = 32
CONF_HALO = 32
CONV_HALO = 8


def _const_spec(shape):
    nd = len(shape)
    return pl.BlockSpec(shape, lambda i: (0,) * nd, pipeline_mode=pl.Buffered(1))


def _row_spec(tm, width):
    return pl.BlockSpec((tm, width), lambda i: (i, 0))


def _sigmoid(v):
    return 1.0 / (1.0 + jnp.exp(-v))


def _silu(v):
    return v * _sigmoid(v)


def _mod_norm(x, gain, scale, shift):
    ms = jnp.mean(x * x, axis=-1, keepdims=True)
    return (x * lax.rsqrt(ms + NORM_EPS)) * (gain * (1.0 + scale)) + shift


def _dot(a, b):
    return jnp.dot(a, b, preferred_element_type=F32)


def _mod_kernel(c_ref, w_ref, b_ref, o_ref):
    c = c_ref[...]
    s = _silu(c)
    o_ref[0] = jnp.sum(w_ref[0] * s, axis=0, keepdims=True) + b_ref[0]


def _modulation(c, w_mod, b_mod):
    depth, d, n = w_mod.shape
    nb = 1536
    assert n % nb == 0
    return pl.pallas_call(
        _mod_kernel,
        grid=(depth, n // nb),
        in_specs=[
            pl.BlockSpec((d, 1), lambda i, j: (0, 0)),
            pl.BlockSpec((1, d, nb), lambda i, j: (i, 0, j)),
            pl.BlockSpec((1, 1, nb), lambda i, j: (i, 0, j)),
        ],
        out_specs=pl.BlockSpec((1, 1, nb), lambda i, j: (i, 0, j)),
        out_shape=jax.ShapeDtypeStruct((depth, 1, n), F32),
        compiler_params=pltpu.CompilerParams(
            dimension_semantics=("arbitrary", "arbitrary"), vmem_limit_bytes=VMEM_LIMIT_BYTES),
        name="mod",
    )(c.reshape(d, 1), w_mod, b_mod.reshape(depth, 1, n))


def _mix0_kernel(x_ref, pos_ref, mod_ref, nm_ref, freq_ref, win_ref, bin_ref, sink_ref, cw_ref, cb_ref,
                 lng_ref, lnb_ref, wout_ref, o_ref, kbuf, vbuf, ush, merged, *, tm):
    i = pl.program_id(0)

    @pl.when(i == 0)
    def _():
        kbuf[:, 0:WINDOW, :] = jnp.zeros((4, WINDOW, LANES), BF16)
        vbuf[:, 0:WINDOW, :] = jnp.zeros((4, WINDOW, LANES), BF16)
        ush[0, 0:CONF_HALO, :] = jnp.zeros((CONF_HALO, CONF_WIDTH), F32)

    x = x_ref[...]
    shift = mod_ref[:, 0:D_MODEL]
    scale = mod_ref[:, D_MODEL:2 * D_MODEL]
    gate = mod_ref[:, 2 * D_MODEL:3 * D_MODEL]
    h = _mod_norm(x, nm_ref[...], scale, shift).astype(BF16)
    proj = _dot(h, win_ref[...]) + bin_ref[...]

    ang = pos_ref[...].astype(F32) * freq_ref[...]
    cosv = jnp.cos(ang)
    sinv = jnp.sin(ang)
    lane = lax.broadcasted_iota(jnp.int32, (1, LANES), 1)
    hl = lane & (HEAD_DIM - 1)
    sin_up = jnp.where((hl >= ROT_DIM // 2) & (hl < ROT_DIM), sinv, 0.0)
    sin_dn = jnp.where(hl < ROT_DIM // 2, -sinv, 0.0)
    lo = lane < HEAD_DIM

    def rope(t):
        return (t * cosv + pltpu.roll(t, ROT_DIM // 2, 1) * sin_up
                + pltpu.roll(t, LANES - ROT_DIM // 2, 1) * sin_dn)

    k = rope(proj[:, ATTN_WIDTH:ATTN_WIDTH + KV_WIDTH])
    v = proj[:, ATTN_WIDTH + KV_WIDTH:ATTN_WIDTH + 2 * KV_WIDTH]
    for buf, t in ((kbuf, k), (vbuf, v)):
        t_sw = pltpu.roll(t, HEAD_DIM, 1)
        buf[0, WINDOW:WINDOW + tm, :] = jnp.where(lo, t, 0.0).astype(BF16)
        buf[1, WINDOW:WINDOW + tm, :] = jnp.where(lo, 0.0, t_sw).astype(BF16)
        buf[2, WINDOW:WINDOW + tm, :] = jnp.where(lo, t_sw, 0.0).astype(BF16)
        buf[3, WINDOW:WINDOW + tm, :] = jnp.where(lo, 0.0, t).astype(BF16)

    q_slabs = [rope(proj[:, s * LANES:(s + 1) * LANES] * (HEAD_DIM ** -0.5)).astype(BF16)
               for s in range(ATTN_WIDTH // LANES)]

    row = lax.broadcasted_iota(jnp.int32, (2 * WINDOW, 1), 0)
    qi = row & (WINDOW - 1)
    kj = lax.broadcasted_iota(jnp.int32, (1, 2 * WINDOW), 1)
    band = (kj > qi) & (kj <= qi + WINDOW)
    band_first = band & ((kj >= WINDOW) | (i > 0))

    for b in range(tm // WINDOW):
        r0 = b * WINDOW
        mask = band_first if b == 0 else band
        for kh in range(N_KV_HEADS):
            qs = jnp.concatenate([q_slabs[2 * kh][r0:r0 + WINDOW], q_slabs[2 * kh + 1][r0:r0 + WINDOW]], axis=0)
            acc = None
            for par in range(2):
                sink = jnp.where(row < WINDOW, sink_ref[4 * kh + par], sink_ref[4 * kh + 2 + par])
                kk = kbuf[2 * kh + par, r0:r0 + 2 * WINDOW, :]
                s = lax.dot_general(qs, kk, (((1,), (1,)), ((), ())), preferred_element_type=F32)
                s = jnp.where(mask, s, -jnp.inf)
                m = jnp.maximum(jnp.max(s, axis=-1, keepdims=True), sink)
                p = jnp.exp(s - m)
                denom = jnp.sum(p, axis=-1, keepdims=True) + jnp.exp(sink - m)
                o = _dot(p.astype(BF16), vbuf[2 * kh + par, r0:r0 + 2 * WINDOW, :]) * (1.0 / denom)
                acc = o if acc is None else acc + o
            merged[r0:r0 + WINDOW, 2 * kh * LANES:(2 * kh + 1) * LANES] = acc[0:WINDOW].astype(BF16)
            merged[r0:r0 + WINDOW, (2 * kh + 1) * LANES:(2 * kh + 2) * LANES] = acc[WINDOW:].astype(BF16)

    kbuf[:, 0:WINDOW, :] = kbuf[:, tm:tm + WINDOW, :]
    vbuf[:, 0:WINDOW, :] = vbuf[:, tm:tm + WINDOW, :]

    c0 = ATTN_WIDTH + 2 * KV_WIDTH
    ush[0, CONF_HALO:CONF_HALO + tm, :] = proj[:, c0:c0 + CONF_WIDTH] * _sigmoid(proj[:, c0 + CONF_WIDTH:])
    n_sh = CONF_HALO + tm - SUBLANES
    for r in range(1, SUBLANES):
        ush[r, 0:n_sh, :] = ush[0, r:r + n_sh, :]
    tap0 = CONF_HALO - (CONF_KERNEL - 1)

    def conf_chunk(ci, carry):
        r0 = pl.multiple_of(ci * CONF_ROWS, CONF_ROWS)
        acc = jnp.zeros((CONF_ROWS, CONF_WIDTH), F32) + cb_ref[...]
        for kk in range(CONF_KERNEL):
            off = tap0 + kk
            win = ush[off % SUBLANES, pl.ds(r0 + (off // SUBLANES) * SUBLANES, CONF_ROWS), :]
            acc = acc + cw_ref[kk:kk + 1, :] * win
        mu = jnp.mean(acc, axis=-1, keepdims=True)
        cen = acc - mu
        var = jnp.mean(cen * cen, axis=-1, keepdims=True)
        hn = cen * lax.rsqrt(var + LN_EPS) * lng_ref[...] + lnb_ref[...]
        merged[pl.ds(r0, CONF_ROWS), ATTN_WIDTH:] = _silu(hn).astype(BF16)
        return carry

    lax.fori_loop(0, tm // CONF_ROWS, conf_chunk, 0, unroll=4)
    ush[0, 0:CONF_HALO, :] = ush[0, tm:tm + CONF_HALO, :]

    y = _dot(merged[...], wout_ref[...])
    o_ref[...] = x + gate * y


def _mix0(x, pos, mod, norm_g, freq, w_in, b_in, sinks, conv_w, conv_b, ln_g, ln_b, w_out):
    s = x.shape[0]
    tm = TM_MIX0
    assert s % tm == 0 and tm % WINDOW == 0 and tm % CONF_ROWS == 0
    return pl.pallas_call(
        functools.partial(_mix0_kernel, tm=tm),
        grid=(s // tm,),
        in_specs=[
            _row_spec(tm, D_MODEL),
            _row_spec(tm, 1),
            _const_spec(mod.shape),
            _const_spec(norm_g.shape),
            _const_spec(freq.shape),
            _const_spec(w_in.shape),
            _const_spec(b_in.shape),
            pl.BlockSpec(memory_space=pltpu.SMEM),
            _const_spec(conv_w.shape),
            _const_spec(conv_b.shape),
            _const_spec(ln_g.shape),
            _const_spec(ln_b.shape),
            _const_spec(w_out.shape),
        ],
        out_specs=_row_spec(tm, D_MODEL),
        out_shape=jax.ShapeDtypeStruct((s, D_MODEL), F32),
        scratch_shapes=[
            pltpu.VMEM((4, WINDOW + tm, LANES), BF16),
            pltpu.VMEM((4, WINDOW + tm, LANES), BF16),
            pltpu.VMEM((SUBLANES, CONF_HALO + tm, CONF_WIDTH), F32),
            pltpu.VMEM((tm, D_MODEL), BF16),
        ],
        compiler_params=pltpu.CompilerParams(
            dimension_semantics=("arbitrary",), vmem_limit_bytes=VMEM_LIMIT_BYTES),
        name="mix0",
    )(x, pos, mod, norm_g, freq, w_in, b_in, sinks, conv_w, conv_b, ln_g, ln_b, w_out)


def _ffn_kernel(x_ref, mod_ref, nf_ref, wg_ref, wv_ref, cw_ref, cb_ref, wd_ref, fn_ref, o_ref, gext,
                *, tm, final_norm):
    i = pl.program_id(0)

    @pl.when(i == 0)
    def _():
        gext[0:CONV_HALO, :] = jnp.zeros((CONV_HALO, D_FF), F32)

    x = x_ref[...]
    shift = mod_ref[:, 3 * D_MODEL:4 * D_MODEL]
    scale = mod_ref[:, 4 * D_MODEL:5 * D_MODEL]
    gate = mod_ref[:, 5 * D_MODEL:6 * D_MODEL]
    h = _mod_norm(x, nf_ref[...], scale, shift).astype(BF16)
    gext[CONV_HALO:CONV_HALO + tm, :] = _dot(h, wg_ref[...])
    val = _dot(h, wv_ref[...])
    g = cb_ref[...]
    for kk in range(FFN_CONV):
        r = CONV_HALO - (FFN_CONV - 1) + kk
        g = g + cw_ref[kk:kk + 1, :] * gext[r:r + tm, :]
    gext[0:CONV_HALO, :] = gext[tm:tm + CONV_HALO, :]
    act = (_silu(g) * val).astype(BF16)
    y = x + gate * _dot(act, wd_ref[...])
    if final_norm:
        ms = jnp.mean(y * y, axis=-1, keepdims=True)
        y = y * lax.rsqrt(ms + NORM_EPS) * fn_ref[...]
    o_ref[...] = y


def _ffn(x, mod, norm_g, w_gate, conv_w, conv_b, w_val, w_down, final_g, final_norm):
    s = x.shape[0]
    tm = TM_FFN
    assert s % tm == 0
    return pl.pallas_call(
        functools.partial(_ffn_kernel, tm=tm, final_norm=final_norm),
        grid=(s // tm,),
        in_specs=[
            _row_spec(tm, D_MODEL),
            _const_spec(mod.shape),
            _const_spec(norm_g.shape),
            _const_spec(w_gate.shape),
            _const_spec(w_val.shape),
            _const_spec(conv_w.shape),
            _const_spec(conv_b.shape),
            _const_spec(w_down.shape),
            _const_spec(final_g.shape),
        ],
        out_specs=_row_spec(tm, D_MODEL),
        out_shape=jax.ShapeDtypeStruct((s, D_MODEL), F32),
        scratch_shapes=[pltpu.VMEM((CONV_HALO + tm, D_FF), F32)],
        compiler_params=pltpu.CompilerParams(
            dimension_semantics=("arbitrary",), vmem_limit_bytes=VMEM_LIMIT_BYTES),
        name="ffn_final" if final_norm else "ffn",
    )(x, mod, norm_g, w_gate, w_val, conv_w, conv_b, w_down, final_g)


def _split3(v):
    hi = v.astype(BF16)
    r1 = v - hi.astype(F32)
    mid = r1.astype(BF16)
    lo = (r1 - mid.astype(F32)).astype(BF16)
    return hi, mid, lo


def _ssd_kernel(x_ref, mod_ref, nm_ref, wz_ref, wx_ref, wdt_ref, cw_ref, cb_ref, dtb_ref, alog_ref, dskip_ref,
                ng_ref, wo_ref, o_ref, xext, xact, dts, adts, state, yscr, *, tm):
    i = pl.program_id(0)

    @pl.when(i == 0)
    def _():
        xext[0:CONV_HALO, :] = jnp.zeros((CONV_HALO, XBC_WIDTH), F32)
        state[...] = jnp.zeros((D_STATE, D_INNER), F32)

    x = x_ref[...]
    shift = mod_ref[:, 0:D_MODEL]
    scale = mod_ref[:, D_MODEL:2 * D_MODEL]
    gate = mod_ref[:, 2 * D_MODEL:3 * D_MODEL]
    h = _mod_norm(x, nm_ref[...], scale, shift).astype(BF16)

    xext[CONV_HALO:CONV_HALO + tm, :] = _dot(h, wx_ref[...])
    xbc = cb_ref[...]
    for kk in range(SSM_CONV):
        r = CONV_HALO - (SSM_CONV - 1) + kk
        xbc = xbc + cw_ref[kk:kk + 1, :] * xext[r:r + tm, :]
    xext[0:CONV_HALO, :] = xext[tm:tm + CONV_HALO, :]
    xact[...] = _silu(xbc)

    dt_in = _dot(h, wdt_ref[...]) + dtb_ref[...]
    dt = jnp.maximum(dt_in, 0.0) + jnp.log1p(jnp.exp(-jnp.abs(dt_in)))
    dts[...] = dt
    adts[...] = dt * (-jnp.exp(alog_ref[...]))

    ri = lax.broadcasted_iota(jnp.int32, (CHUNK, CHUNK), 0)
    ci = lax.broadcasted_iota(jnp.int32, (CHUNK, CHUNK), 1)
    causal = ri >= ci
    tril = jnp.where(causal, 1.0, 0.0).astype(BF16)
    lo = lax.broadcasted_iota(jnp.int32, (1, LANES), 1) < SSM_HEADDIM
    b0 = D_INNER
    c0 = D_INNER + SSM_GROUPS * D_STATE

    def chunk_body(c, carry):
        r = pl.multiple_of(c * CHUNK, CHUNK)
        dt_t = dts[pl.ds(r, CHUNK), :].T
        hi, mid, low = _split3(adts[pl.ds(r, CHUNK), :])
        a_cs = _dot(tril, hi) + _dot(tril, mid) + _dot(tril, low)
        a_cs_t = a_cs.T
        w_end_t = dt_t * jnp.exp(a_cs_t[:, CHUNK - 1:CHUNK] - a_cs_t)
        cdec = jnp.exp(a_cs[CHUNK - 1:CHUNK, :])
        for g in range(SSM_GROUPS):
            bm_t = xact[pl.ds(r, CHUNK), b0 + g * D_STATE:b0 + (g + 1) * D_STATE].T
            cm_g = xact[pl.ds(r, CHUNK), c0 + g * D_STATE:c0 + (g + 1) * D_STATE]
            cb = _dot(cm_g.astype(BF16), bm_t.astype(BF16))
            for jp in range(HEADS_PER_GROUP // 2):
                sl = g * (HEADS_PER_GROUP // 2) + jp
                xs_p = xact[pl.ds(r, CHUNK), sl * LANES:(sl + 1) * LANES]
                st_p = state[:, sl * LANES:(sl + 1) * LANES]
                y_p = None
                s_new = None
                for par in range(2):
                    hd = 2 * sl + par
                    keep = lo if par == 0 else jnp.logical_not(lo)
                    xs_m = jnp.where(keep, xs_p, 0.0).astype(BF16)
                    st_m = jnp.where(keep, st_p, 0.0).astype(BF16)
                    a_l = jnp.broadcast_to(a_cs[:, hd:hd + 1], (CHUNK, CHUNK))
                    a_s = a_cs_t[hd:hd + 1, :]
                    decay = jnp.exp(jnp.where(causal, a_l - a_s, -jnp.inf))
                    m_in = cb * decay * dt_t[hd:hd + 1, :]
                    c_off = cm_g * jnp.exp(a_l)
                    lhs = jnp.concatenate([m_in, c_off], axis=1).astype(BF16)
                    rhs = jnp.concatenate([xs_m, st_m], axis=0)
                    y_h = _dot(lhs, rhs)
                    b_dec_t = (bm_t * w_end_t[hd:hd + 1, :]).astype(BF16)
                    s_h = _dot(b_dec_t, xs_m)
                    y_p = y_h if y_p is None else y_p + y_h
                    s_new = s_h if s_new is None else s_new + s_h
                cd_p = jnp.where(lo, cdec[:, 2 * sl:2 * sl + 1], cdec[:, 2 * sl + 1:2 * sl + 2])
                state[:, sl * LANES:(sl + 1) * LANES] = cd_p * st_p + s_new
                yscr[pl.ds(r, CHUNK), sl * LANES:(sl + 1) * LANES] = (
                    y_p + xs_p * dskip_ref[:, sl * LANES:(sl + 1) * LANES])
        return carry

    lax.fori_loop(0, tm // CHUNK, chunk_body, 0)

    z = _dot(h, wz_ref[...])
    y = yscr[...] * _silu(z)
    gw = D_INNER // SSM_GROUPS
    parts = []
    for g in range(SSM_GROUPS):
        yg = y[:, g * gw:(g + 1) * gw]
        ms = jnp.mean(yg * yg, axis=-1, keepdims=True)
        parts.append((yg * lax.rsqrt(ms + NORM_EPS) * ng_ref[:, g * gw:(g + 1) * gw]).astype(BF16))
    yn = jnp.concatenate(parts, axis=1)
    o_ref[...] = x + gate * _dot(yn, wo_ref[...])


def _ssd(x, mod, norm_g, w_z, w_x, w_dt, conv_w, conv_b, dt_bias, a_log, d_skip, norm_y, w_out):
    s = x.shape[0]
    tm = TM_SSD
    assert s % tm == 0 and tm % CHUNK == 0
    return pl.pallas_call(
        functools.partial(_ssd_kernel, tm=tm),
        grid=(s // tm,),
        in_specs=[
            _row_spec(tm, D_MODEL),
            _const_spec(mod.shape),
            _const_spec(norm_g.shape),
            _const_spec(w_z.shape),
            _const_spec(w_x.shape),
            _const_spec(w_dt.shape),
            _const_spec(conv_w.shape),
            _const_spec(conv_b.shape),
            _const_spec(dt_bias.shape),
            _const_spec(a_log.shape),
            _const_spec(d_skip.shape),
            _const_spec(norm_y.shape),
            _const_spec(w_out.shape),
        ],
        out_specs=_row_spec(tm, D_MODEL),
        out_shape=jax.ShapeDtypeStruct((s, D_MODEL), F32),
        scratch_shapes=[
            pltpu.VMEM((CONV_HALO + tm, XBC_WIDTH), F32),
            pltpu.VMEM((tm, XBC_WIDTH), F32),
            pltpu.VMEM((tm, LANES), F32),
            pltpu.VMEM((tm, LANES), F32),
            pltpu.VMEM((D_STATE, D_INNER), F32),
            pltpu.VMEM((tm, D_INNER), F32),
        ],
        compiler_params=pltpu.CompilerParams(
            dimension_semantics=("arbitrary",), vmem_limit_bytes=VMEM_LIMIT_BYTES),
        name="ssd",
    )(x, mod, norm_g, w_z, w_x, w_dt, conv_w, conv_b, dt_bias, a_log, d_skip, norm_y, w_out)


def _pad_lanes(v, width=LANES):
    return jnp.pad(v, ((0, 0), (0, width - v.shape[1])))


def kernel(x, c, positions, w_mod, b_mod, norm_mix, norm_ffn, w_in_e, b_in_e, attn_sinks, conf_conv_w,
           conf_conv_b, conf_ln_g, conf_ln_b, w_out_e, w_in_o, ssm_conv_w, ssm_conv_b, ssm_dt_bias, ssm_a_log,
           ssm_d, ssm_norm_g, w_out_o, ffn_w_gate, ffn_conv_w, ffn_conv_b, ffn_w_val, ffn_w_down, final_norm):
    b, s, d = x.shape
    assert b == 1 and d == D_MODEL and w_mod.shape[0] == 2
    xs = x.reshape(s, d)
    pos = positions.reshape(s, 1)

    mod = _modulation(c, w_mod, b_mod)

    half = ROT_DIM // 2
    inv_freq = 1.0 / (ROPE_THETA ** (jnp.arange(half, dtype=F32) * 2.0 / ROT_DIM))
    head_freq = jnp.concatenate([inv_freq, inv_freq, jnp.zeros((HEAD_DIM - ROT_DIM,), F32)])
    freq = jnp.tile(head_freq, LANES // HEAD_DIM).reshape(1, LANES)

    row = lambda v: v.reshape(1, -1)
    final_g = row(final_norm)

    xs = _mix0(xs, pos, mod[0], row(norm_mix[0]), freq, w_in_e[0].astype(BF16), row(b_in_e[0]), attn_sinks[0],
               jnp.pad(conf_conv_w[0], ((0, 1), (0, 0))), row(conf_conv_b[0]), row(conf_ln_g[0]),
               row(conf_ln_b[0]), w_out_e[0].astype(BF16))
    xs = _ffn(xs, mod[0], row(norm_ffn[0]), ffn_w_gate[0].astype(BF16), ffn_conv_w[0], row(ffn_conv_b[0]),
              ffn_w_val[0].astype(BF16), ffn_w_down[0].astype(BF16), final_g, False)

    w_in = w_in_o[0]
    w_z = w_in[:, :D_INNER].astype(BF16)
    w_x = w_in[:, D_INNER:D_INNER + XBC_WIDTH].astype(BF16)
    w_dt = _pad_lanes(w_in[:, D_INNER + XBC_WIDTH:]).astype(BF16)
    d_skip = jnp.repeat(ssm_d[0], SSM_HEADDIM).reshape(1, D_INNER)
    xs = _ssd(xs, mod[1], row(norm_mix[1]), w_z, w_x, w_dt, ssm_conv_w[0], row(ssm_conv_b[0]),
              _pad_lanes(row(ssm_dt_bias[0])), _pad_lanes(row(ssm_a_log[0])), d_skip, row(ssm_norm_g[0]),
              w_out_o[0].astype(BF16))
    xs = _ffn(xs, mod[1], row(norm_ffn[1]), ffn_w_gate[1].astype(BF16), ffn_conv_w[1], row(ffn_conv_b[1]),
              ffn_w_val[1].astype(BF16), ffn_w_down[1].astype(BF16), final_g, True)
    return xs.reshape(b, s, d)
```

```python
import functools

import jax
import jax.numpy as jnp
from jax import lax
from jax.experimental import pallas as pl
from jax.experimental.pallas import tpu as pltpu

F32 = jnp.float32
BF16 = jnp.bfloat16

D_MODEL = 1024
N_Q_HEADS = 8
N_KV_HEADS = 2
HEAD_DIM = 64
ATTN_WIDTH = N_Q_HEADS * HEAD_DIM
KV_WIDTH = N_KV_HEADS * HEAD_DIM
WINDOW = 128
ROT_DIM = HEAD_DIM // 4
ROPE_THETA = 500000.0
CONF_WIDTH = D_MODEL // 2
CONF_KERNEL = 31
E_IN_COLS = ATTN_WIDTH + 2 * KV_WIDTH + 2 * CONF_WIDTH
D_INNER = 2 * D_MODEL
SSM_HEADDIM = 64
SSM_HEADS = D_INNER // SSM_HEADDIM
SSM_GROUPS = 4
HEADS_PER_GROUP = SSM_HEADS // SSM_GROUPS
D_STATE = 128
SSM_CONV = 4
CHUNK = 128
XBC_WIDTH = D_INNER + 2 * SSM_GROUPS * D_STATE
D_FF = 2816
FFN_CONV = 3
NORM_EPS = 1e-6
LN_EPS = 1e-5

LANES = 128
SUBLANES = 8
VMEM_LIMIT_BYTES = 56 * 1024 * 1024

TM_MIX0 = 512
TM_FFN = 256
TM_SSD = 256
CONF_ROWS = 32
CONF_HALO = 32
CONV_HALO = 8


def _const_spec(shape):
    nd = len(shape)
    return pl.BlockSpec(shape, lambda i: (0,) * nd, pipeline_mode=pl.Buffered(1))


def _row_spec(tm, width):
    return pl.BlockSpec((tm, width), lambda i: (i, 0))


def _gated(a, g):
    ha = 0.5 * a
    return ha + ha * jnp.tanh(0.5 * g)


def _silu(v):
    return _gated(v, v)


def _slab(s):
    return slice(s * LANES, (s + 1) * LANES)


def _rows2(start, n):
    return pl.ds(start, n, stride=2)


def _mod_norm(x, gain, scale, shift):
    ms = jnp.mean(x * x, axis=-1, keepdims=True)
    return (x * lax.rsqrt(ms + NORM_EPS)) * (gain * (1.0 + scale)) + shift


def _dot(a, b):
    return jnp.dot(a, b, preferred_element_type=F32)


def _mod_kernel(c_ref, w_ref, b_ref, o_ref):
    c = c_ref[...]
    s = _silu(c)
    o_ref[0] = jnp.sum(w_ref[0] * s, axis=0, keepdims=True) + b_ref[0]


def _modulation(c, w_mod, b_mod):
    depth, d, n = w_mod.shape
    nb = 1536
    assert n % nb == 0
    return pl.pallas_call(
        _mod_kernel,
        grid=(depth, n // nb),
        in_specs=[
            pl.BlockSpec((d, 1), lambda i, j: (0, 0)),
            pl.BlockSpec((1, d, nb), lambda i, j: (i, 0, j)),
            pl.BlockSpec((1, 1, nb), lambda i, j: (i, 0, j)),
        ],
        out_specs=pl.BlockSpec((1, 1, nb), lambda i, j: (i, 0, j)),
        out_shape=jax.ShapeDtypeStruct((depth, 1, n), F32),
        compiler_params=pltpu.CompilerParams(
            dimension_semantics=("arbitrary", "arbitrary"), vmem_limit_bytes=VMEM_LIMIT_BYTES),
        name="mod",
    )(c.reshape(d, 1), w_mod, b_mod.reshape(depth, 1, n))


def _mix0_kernel(x_ref, pos_ref, mod_ref, nm_ref, freq_ref, win_ref, bin_ref, sink_ref, cw_ref, cb_ref,
                 lng_ref, lnb_ref, wout_ref, o_ref, kbuf, vbuf, uslab, hslab, merged, *, tm):
    i = pl.program_id(0)

    @pl.when(i == 0)
    def _():
        kbuf[:, 0:WINDOW, :] = jnp.zeros((4, WINDOW, LANES), BF16)
        vbuf[:, 0:WINDOW, :] = jnp.zeros((4, WINDOW, LANES), BF16)
        uslab[:, 0:CONF_HALO, :] = jnp.zeros((CONF_WIDTH // LANES, CONF_HALO, LANES), F32)

    x = x_ref[...]
    shift = mod_ref[:, 0:D_MODEL]
    scale = mod_ref[:, D_MODEL:2 * D_MODEL]
    gate = mod_ref[:, 2 * D_MODEL:3 * D_MODEL]
    h = _mod_norm(x, nm_ref[...], scale, shift).astype(BF16)
    proj = _dot(h, win_ref[...]) + bin_ref[...]

    ang = pos_ref[...].astype(F32) * freq_ref[...]
    cosv = jnp.cos(ang)
    sinv = jnp.sin(ang)
    lane = lax.broadcasted_iota(jnp.int32, (1, LANES), 1)
    hl = lane & (HEAD_DIM - 1)
    sin_up = jnp.where((hl >= ROT_DIM // 2) & (hl < ROT_DIM), sinv, 0.0)
    sin_dn = jnp.where(hl < ROT_DIM // 2, -sinv, 0.0)
    lo = lane < HEAD_DIM

    def rope(t):
        return (t * cosv + pltpu.roll(t, ROT_DIM // 2, 1) * sin_up
                + pltpu.roll(t, LANES - ROT_DIM // 2, 1) * sin_dn)

    k = rope(proj[:, ATTN_WIDTH:ATTN_WIDTH + KV_WIDTH])
    v = proj[:, ATTN_WIDTH + KV_WIDTH:ATTN_WIDTH + 2 * KV_WIDTH]
    for buf, t in ((kbuf, k), (vbuf, v)):
        t_sw = pltpu.roll(t, HEAD_DIM, 1)
        buf[0, WINDOW:WINDOW + tm, :] = jnp.where(lo, t, 0.0).astype(BF16)
        buf[1, WINDOW:WINDOW + tm, :] = jnp.where(lo, 0.0, t_sw).astype(BF16)
        buf[2, WINDOW:WINDOW + tm, :] = jnp.where(lo, t_sw, 0.0).astype(BF16)
        buf[3, WINDOW:WINDOW + tm, :] = jnp.where(lo, 0.0, t).astype(BF16)

    q_slabs = [rope(proj[:, s * LANES:(s + 1) * LANES] * (HEAD_DIM ** -0.5)).astype(BF16)
               for s in range(ATTN_WIDTH // LANES)]

    row = lax.broadcasted_iota(jnp.int32, (2 * WINDOW, 1), 0)
    qi = row & (WINDOW - 1)
    kj = lax.broadcasted_iota(jnp.int32, (1, 2 * WINDOW), 1)
    band = (kj > qi) & (kj <= qi + WINDOW)
    band_first = band & ((kj >= WINDOW) | (i > 0))

    for b in range(tm // WINDOW):
        r0 = b * WINDOW
        mask = band_first if b == 0 else band
        for kh in range(N_KV_HEADS):
            qs = jnp.concatenate([q_slabs[2 * kh][r0:r0 + WINDOW], q_slabs[2 * kh + 1][r0:r0 + WINDOW]], axis=0)
            acc = None
            for par in range(2):
                sink = jnp.where(row < WINDOW, sink_ref[4 * kh + par], sink_ref[4 * kh + 2 + par])
                kk = kbuf[2 * kh + par, r0:r0 + 2 * WINDOW, :]
                s = lax.dot_general(qs, kk, (((1,), (1,)), ((), ())), preferred_element_type=F32)
                s = jnp.where(mask, s, -jnp.inf)
                m = jnp.maximum(jnp.max(s, axis=-1, keepdims=True), sink)
                p = jnp.exp(s - m)
                denom = jnp.sum(p, axis=-1, keepdims=True) + jnp.exp(sink - m)
                o = _dot(p.astype(BF16), vbuf[2 * kh + par, r0:r0 + 2 * WINDOW, :]) * (1.0 / denom)
                acc = o if acc is None else acc + o
            merged[r0:r0 + WINDOW, 2 * kh * LANES:(2 * kh + 1) * LANES] = acc[0:WINDOW].astype(BF16)
            merged[r0:r0 + WINDOW, (2 * kh + 1) * LANES:(2 * kh + 2) * LANES] = acc[WINDOW:].astype(BF16)

    kbuf[:, 0:WINDOW, :] = kbuf[:, tm:tm + WINDOW, :]
    vbuf[:, 0:WINDOW, :] = vbuf[:, tm:tm + WINDOW, :]

    c0 = ATTN_WIDTH + 2 * KV_WIDTH
    n_cs = CONF_WIDTH // LANES
    for s in range(n_cs):
        uslab[s, CONF_HALO:CONF_HALO + tm, :] = _gated(proj[:, c0 + s * LANES:c0 + (s + 1) * LANES],
                                                      proj[:, c0 + CONF_WIDTH + s * LANES:
                                                           c0 + CONF_WIDTH + (s + 1) * LANES])
    tap0 = CONF_HALO - (CONF_KERNEL - 1)
    half = CONF_ROWS // 2

    def conf_chunk(ci, carry):
        r0 = pl.multiple_of(ci * CONF_ROWS, CONF_ROWS)
        for par in range(2):
            accs = []
            for s in range(n_cs):
                acc = jnp.zeros((half, LANES), F32) + cb_ref[:, _slab(s)]
                for kk in range(CONF_KERNEL):
                    acc = acc + cw_ref[kk:kk + 1, _slab(s)] * uslab[s, _rows2(r0 + tap0 + kk + par, half), :]
                accs.append(acc)
            mu = sum(jnp.sum(a, axis=-1, keepdims=True) for a in accs) * (1.0 / CONF_WIDTH)
            cens = [a - mu for a in accs]
            var = sum(jnp.sum(cn * cn, axis=-1, keepdims=True) for cn in cens) * (1.0 / CONF_WIDTH)
            rstd = lax.rsqrt(var + LN_EPS)
            for s in range(n_cs):
                hn = cens[s] * rstd * lng_ref[:, _slab(s)] + lnb_ref[:, _slab(s)]
                hslab[s, _rows2(r0 + par, half), :] = _silu(hn)
        return carry

    lax.fori_loop(0, tm // CONF_ROWS, conf_chunk, 0, unroll=4)
    uslab[:, 0:CONF_HALO, :] = uslab[:, tm:tm + CONF_HALO, :]
    for s in range(n_cs):
        merged[:, ATTN_WIDTH + s * LANES:ATTN_WIDTH + (s + 1) * LANES] = hslab[s].astype(BF16)

    y = _dot(merged[...], wout_ref[...])
    o_ref[...] = x + gate * y


def _mix0(x, pos, mod, norm_g, freq, w_in, b_in, sinks, conv_w, conv_b, ln_g, ln_b, w_out):
    s = x.shape[0]
    tm = TM_MIX0
    assert s % tm == 0 and tm % WINDOW == 0 and tm % CONF_ROWS == 0
    return pl.pallas_call(
        functools.partial(_mix0_kernel, tm=tm),
        grid=(s // tm,),
        in_specs=[
            _row_spec(tm, D_MODEL),
            _row_spec(tm, 1),
            _const_spec(mod.shape),
            _const_spec(norm_g.shape),
            _const_spec(freq.shape),
            _const_spec(w_in.shape),
            _const_spec(b_in.shape),
            pl.BlockSpec(memory_space=pltpu.SMEM),
            _const_spec(conv_w.shape),
            _const_spec(conv_b.shape),
            _const_spec(ln_g.shape),
            _const_spec(ln_b.shape),
            _const_spec(w_out.shape),
        ],
        out_specs=_row_spec(tm, D_MODEL),
        out_shape=jax.ShapeDtypeStruct((s, D_MODEL), F32),
        scratch_shapes=[
            pltpu.VMEM((4, WINDOW + tm, LANES), BF16),
            pltpu.VMEM((4, WINDOW + tm, LANES), BF16),
            pltpu.VMEM((CONF_WIDTH // LANES, CONF_HALO + tm, LANES), F32),
            pltpu.VMEM((CONF_WIDTH // LANES, tm, LANES), F32),
            pltpu.VMEM((tm, D_MODEL), BF16),
        ],
        compiler_params=pltpu.CompilerParams(
            dimension_semantics=("arbitrary",), vmem_limit_bytes=VMEM_LIMIT_BYTES),
        name="mix0",
    )(x, pos, mod, norm_g, freq, w_in, b_in, sinks, conv_w, conv_b, ln_g, ln_b, w_out)


def _ffn_kernel(x_ref, mod_ref, nf_ref, wg_ref, wv_ref, cw_ref, cb_ref, wd_ref, fn_ref, o_ref, gslab, sgslab,
                *, tm, final_norm):
    i = pl.program_id(0)

    @pl.when(i == 0)
    def _():
        gslab[:, 0:CONV_HALO, :] = jnp.zeros((D_FF // LANES, CONV_HALO, LANES), F32)

    x = x_ref[...]
    shift = mod_ref[:, 3 * D_MODEL:4 * D_MODEL]
    scale = mod_ref[:, 4 * D_MODEL:5 * D_MODEL]
    gate = mod_ref[:, 5 * D_MODEL:6 * D_MODEL]
    h = _mod_norm(x, nf_ref[...], scale, shift).astype(BF16)
    n_s = D_FF // LANES
    graw = _dot(h, wg_ref[...])
    for s in range(n_s):
        gslab[s, CONV_HALO:CONV_HALO + tm, :] = graw[:, _slab(s)]
    val = _dot(h, wv_ref[...])
    tap0 = CONV_HALO - (FFN_CONV - 1)
    for s in range(n_s):
        for par in range(2):
            g = cb_ref[:, _slab(s)]
            for kk in range(FFN_CONV):
                g = g + cw_ref[kk:kk + 1, _slab(s)] * gslab[s, _rows2(tap0 + kk + par, tm // 2), :]
            sgslab[s, _rows2(par, tm // 2), :] = _silu(g)
    gslab[:, 0:CONV_HALO, :] = gslab[:, tm:tm + CONV_HALO, :]
    act = (jnp.concatenate([sgslab[s] for s in range(n_s)], axis=1) * val).astype(BF16)
    y = x + gate * _dot(act, wd_ref[...])
    if final_norm:
        ms = jnp.mean(y * y, axis=-1, keepdims=True)
        y = y * lax.rsqrt(ms + NORM_EPS) * fn_ref[...]
    o_ref[...] = y


def _ffn(x, mod, norm_g, w_gate, conv_w, conv_b, w_val, w_down, final_g, final_norm):
    s = x.shape[0]
    tm = TM_FFN
    assert s % tm == 0
    return pl.pallas_call(
        functools.partial(_ffn_kernel, tm=tm, final_norm=final_norm),
        grid=(s // tm,),
        in_specs=[
            _row_spec(tm, D_MODEL),
            _const_spec(mod.shape),
            _const_spec(norm_g.shape),
            _const_spec(w_gate.shape),
            _const_spec(w_val.shape),
            _const_spec(conv_w.shape),
            _const_spec(conv_b.shape),
            _const_spec(w_down.shape),
            _const_spec(final_g.shape),
        ],
        out_specs=_row_spec(tm, D_MODEL),
        out_shape=jax.ShapeDtypeStruct((s, D_MODEL), F32),
        scratch_shapes=[pltpu.VMEM((D_FF // LANES, CONV_HALO + tm, LANES), F32),
                        pltpu.VMEM((D_FF // LANES, tm, LANES), F32)],
        compiler_params=pltpu.CompilerParams(
            dimension_semantics=("arbitrary",), vmem_limit_bytes=VMEM_LIMIT_BYTES),
        name="ffn_final" if final_norm else "ffn",
    )(x, mod, norm_g, w_gate, w_val, conv_w, conv_b, w_down, final_g)


def _split3(v):
    hi = v.astype(BF16)
    r1 = v - hi.astype(F32)
    mid = r1.astype(BF16)
    lo = (r1 - mid.astype(F32)).astype(BF16)
    return hi, mid, lo


def _ssd_kernel(x_ref, mod_ref, nm_ref, wz_ref, wx_ref, wdt_ref, cw_ref, cb_ref, dtb_ref, alog_ref, dskip_ref,
                ng_ref, wo_ref, o_ref, xslab, xact, dts, adts, state, yscr, *, tm):
    i = pl.program_id(0)

    @pl.when(i == 0)
    def _():
        xslab[:, 0:CONV_HALO, :] = jnp.zeros((XBC_WIDTH // LANES, CONV_HALO, LANES), F32)
        state[...] = jnp.zeros((D_STATE, D_INNER), F32)

    x = x_ref[...]
    shift = mod_ref[:, 0:D_MODEL]
    scale = mod_ref[:, D_MODEL:2 * D_MODEL]
    gate = mod_ref[:, 2 * D_MODEL:3 * D_MODEL]
    h = _mod_norm(x, nm_ref[...], scale, shift).astype(BF16)

    n_s = XBC_WIDTH // LANES
    xraw = _dot(h, wx_ref[...])
    for s in range(n_s):
        xslab[s, CONV_HALO:CONV_HALO + tm, :] = xraw[:, _slab(s)]
    tap0 = CONV_HALO - (SSM_CONV - 1)
    for s in range(n_s):
        for par in range(2):
            acc = cb_ref[:, _slab(s)]
            for kk in range(SSM_CONV):
                acc = acc + cw_ref[kk:kk + 1, _slab(s)] * xslab[s, _rows2(tap0 + kk + par, tm // 2), :]
            xact[s, _rows2(par, tm // 2), :] = _silu(acc)
    xslab[:, 0:CONV_HALO, :] = xslab[:, tm:tm + CONV_HALO, :]

    dt_in = _dot(h, wdt_ref[...]) + dtb_ref[...]
    dt = jnp.maximum(dt_in, 0.0) + jnp.log1p(jnp.exp(-jnp.abs(dt_in)))
    dts[...] = dt
    adts[...] = dt * (-jnp.exp(alog_ref[...]))

    ri = lax.broadcasted_iota(jnp.int32, (CHUNK, CHUNK), 0)
    ci = lax.broadcasted_iota(jnp.int32, (CHUNK, CHUNK), 1)
    causal = ri >= ci
    tril = jnp.where(causal, 1.0, 0.0).astype(BF16)
    lo = lax.broadcasted_iota(jnp.int32, (1, LANES), 1) < SSM_HEADDIM
    b0 = D_INNER // LANES
    c0 = b0 + SSM_GROUPS * D_STATE // LANES

    def chunk_body(c, carry):
        r = pl.multiple_of(c * CHUNK, CHUNK)
        dt_t = dts[pl.ds(r, CHUNK), :].T
        hi, mid, low = _split3(adts[pl.ds(r, CHUNK), :])
        a_cs = _dot(tril, hi) + _dot(tril, mid) + _dot(tril, low)
        a_cs_t = a_cs.T
        w_end_t = dt_t * jnp.exp(a_cs_t[:, CHUNK - 1:CHUNK] - a_cs_t)
        cdec = jnp.exp(a_cs[CHUNK - 1:CHUNK, :])
        for g in range(SSM_GROUPS):
            bm_t = xact[b0 + g, pl.ds(r, CHUNK), :].T
            cm_g = xact[c0 + g, pl.ds(r, CHUNK), :]
            cb = _dot(cm_g.astype(BF16), bm_t.astype(BF16))
            for jp in range(HEADS_PER_GROUP // 2):
                sl = g * (HEADS_PER_GROUP // 2) + jp
                xs_p = xact[sl, pl.ds(r, CHUNK), :]
                st_p = state[:, sl * LANES:(sl + 1) * LANES]
                y_p = None
                s_new = None
                for par in range(2):
                    hd = 2 * sl + par
                    keep = lo if par == 0 else jnp.logical_not(lo)
                    xs_m = jnp.where(keep, xs_p, 0.0).astype(BF16)
                    st_m = jnp.where(keep, st_p, 0.0).astype(BF16)
                    a_l = jnp.broadcast_to(a_cs[:, hd:hd + 1], (CHUNK, CHUNK))
                    a_s = a_cs_t[hd:hd + 1, :]
                    decay = jnp.exp(jnp.where(causal, a_l - a_s, -jnp.inf))
                    m_in = cb * decay * dt_t[hd:hd + 1, :]
                    c_off = cm_g * jnp.exp(a_l)
                    lhs = jnp.concatenate([m_in, c_off], axis=1).astype(BF16)
                    rhs = jnp.concatenate([xs_m, st_m], axis=0)
                    y_h = _dot(lhs, rhs)
                    b_dec_t = (bm_t * w_end_t[hd:hd + 1, :]).astype(BF16)
                    s_h = _dot(b_dec_t, xs_m)
                    y_p = y_h if y_p is None else y_p + y_h
                    s_new = s_h if s_new is None else s_new + s_h
                cd_p = jnp.where(lo, cdec[:, 2 * sl:2 * sl + 1], cdec[:, 2 * sl + 1:2 * sl + 2])
                state[:, sl * LANES:(sl + 1) * LANES] = cd_p * st_p + s_new
                yscr[pl.ds(r, CHUNK), sl * LANES:(sl + 1) * LANES] = (
                    y_p + xs_p * dskip_ref[:, sl * LANES:(sl + 1) * LANES])
        return carry

    lax.fori_loop(0, tm // CHUNK, chunk_body, 0, unroll=True)

    z = _dot(h, wz_ref[...])
    y = yscr[...] * _silu(z)
    gw = D_INNER // SSM_GROUPS
    parts = []
    for g in range(SSM_GROUPS):
        yg = y[:, g * gw:(g + 1) * gw]
        ms = jnp.mean(yg * yg, axis=-1, keepdims=True)
        parts.append((yg * lax.rsqrt(ms + NORM_EPS) * ng_ref[:, g * gw:(g + 1) * gw]).astype(BF16))
    yn = jnp.concatenate(parts, axis=1)
    o_ref[...] = x + gate * _dot(yn, wo_ref[...])


def _ssd(x, mod, norm_g, w_z, w_x, w_dt, conv_w, conv_b, dt_bias, a_log, d_skip, norm_y, w_out):
    s = x.shape[0]
    tm = TM_SSD
    assert s % tm == 0 and tm % CHUNK == 0
    return pl.pallas_call(
        functools.partial(_ssd_kernel, tm=tm),
        grid=(s // tm,),
        in_specs=[
            _row_spec(tm, D_MODEL),
            _const_spec(mod.shape),
            _const_spec(norm_g.shape),
            _const_spec(w_z.shape),
            _const_spec(w_x.shape),
            _const_spec(w_dt.shape),
            _const_spec(conv_w.shape),
            _const_spec(conv_b.shape),
            _const_spec(dt_bias.shape),
            _const_spec(a_log.shape),
            _const_spec(d_skip.shape),
            _const_spec(norm_y.shape),
            _const_spec(w_out.shape),
        ],
        out_specs=_row_spec(tm, D_MODEL),
        out_shape=jax.ShapeDtypeStruct((s, D_MODEL), F32),
        scratch_shapes=[
            pltpu.VMEM((XBC_WIDTH // LANES, CONV_HALO + tm, LANES), F32),
            pltpu.VMEM((XBC_WIDTH // LANES, tm, LANES), F32),
            pltpu.VMEM((tm, LANES), F32),
            pltpu.VMEM((tm, LANES), F32),
            pltpu.VMEM((D_STATE, D_INNER), F32),
            pltpu.VMEM((tm, D_INNER), F32),
        ],
        compiler_params=pltpu.CompilerParams(
            dimension_semantics=("arbitrary",), vmem_limit_bytes=VMEM_LIMIT_BYTES),
        name="ssd",
    )(x, mod, norm_g, w_z, w_x, w_dt, conv_w, conv_b, dt_bias, a_log, d_skip, norm_y, w_out)


def _pad_lanes(v, width=LANES):
    return jnp.pad(v, ((0, 0), (0, width - v.shape[1])))


def kernel(x, c, positions, w_mod, b_mod, norm_mix, norm_ffn, w_in_e, b_in_e, attn_sinks, conf_conv_w,
           conf_conv_b, conf_ln_g, conf_ln_b, w_out_e, w_in_o, ssm_conv_w, ssm_conv_b, ssm_dt_bias, ssm_a_log,
           ssm_d, ssm_norm_g, w_out_o, ffn_w_gate, ffn_conv_w, ffn_conv_b, ffn_w_val, ffn_w_down, final_norm):
    b, s, d = x.shape
    assert b == 1 and d == D_MODEL and w_mod.shape[0] == 2
    xs = x.reshape(s, d)
    pos = positions.reshape(s, 1)

    mod = _modulation(c, w_mod, b_mod)

    half = ROT_DIM // 2
    inv_freq = 1.0 / (ROPE_THETA ** (jnp.arange(half, dtype=F32) * 2.0 / ROT_DIM))
    head_freq = jnp.concatenate([inv_freq, inv_freq, jnp.zeros((HEAD_DIM - ROT_DIM,), F32)])
    freq = jnp.tile(head_freq, LANES // HEAD_DIM).reshape(1, LANES)

    row = lambda v: v.reshape(1, -1)
    final_g = row(final_norm)

    xs = _mix0(xs, pos, mod[0], row(norm_mix[0]), freq, w_in_e[0].astype(BF16), row(b_in_e[0]), attn_sinks[0],
               jnp.pad(conf_conv_w[0], ((0, 1), (0, 0))), row(conf_conv_b[0]), row(conf_ln_g[0]),
               row(conf_ln_b[0]), w_out_e[0].astype(BF16))
    xs = _ffn(xs, mod[0], row(norm_ffn[0]), ffn_w_gate[0].astype(BF16), ffn_conv_w[0], row(ffn_conv_b[0]),
              ffn_w_val[0].astype(BF16), ffn_w_down[0].astype(BF16), final_g, False)

    w_in = w_in_o[0]
    w_z = w_in[:, :D_INNER].astype(BF16)
    w_x = w_in[:, D_INNER:D_INNER + XBC_WIDTH].astype(BF16)
    w_dt = _pad_lanes(w_in[:, D_INNER + XBC_WIDTH:]).astype(BF16)
    d_skip = jnp.repeat(ssm_d[0], SSM_HEADDIM).reshape(1, D_INNER)
    xs = _ssd(xs, mod[1], row(norm_mix[1]), w_z, w_x, w_dt, ssm_conv_w[0], row(ssm_conv_b[0]),
              _pad_lanes(row(ssm_dt_bias[0])), _pad_lanes(row(ssm_a_log[0])), d_skip, row(ssm_norm_g[0]),
              w_out_o[0].astype(BF16))
    xs = _ffn(xs, mod[1], row(norm_ffn[1]), ffn_w_gate[1].astype(BF16), ffn_conv_w[1], row(ffn_conv_b[1]),
              ffn_w_val[1].astype(BF16), ffn_w_down[1].astype(BF16), final_g, True)
    return xs.reshape(b, s, d)
```

```python
import functools

import jax
import jax.numpy as jnp
import numpy as np
from jax import lax
from jax.experimental import pallas as pl
from jax.experimental.pallas import tpu as pltpu

F32 = jnp.float32
BF16 = jnp.bfloat16

D_MODEL = 1024
N_Q_HEADS = 8
N_KV_HEADS = 2
HEAD_DIM = 64
ATTN_WIDTH = N_Q_HEADS * HEAD_DIM
KV_WIDTH = N_KV_HEADS * HEAD_DIM
WINDOW = 128
ROT_DIM = HEAD_DIM // 4
ROPE_THETA = 500000.0
CONF_WIDTH = D_MODEL // 2
CONF_KERNEL = 31
E_IN_COLS = ATTN_WIDTH + 2 * KV_WIDTH + 2 * CONF_WIDTH
D_INNER = 2 * D_MODEL
SSM_HEADDIM = 64
SSM_HEADS = D_INNER // SSM_HEADDIM
SSM_GROUPS = 4
HEADS_PER_GROUP = SSM_HEADS // SSM_GROUPS
D_STATE = 128
SSM_CONV = 4
CHUNK = 128
XBC_WIDTH = D_INNER + 2 * SSM_GROUPS * D_STATE
D_FF = 2816
FFN_CONV = 3
NORM_EPS = 1e-6
LN_EPS = 1e-5

LANES = 128
SUBLANES = 8
VMEM_LIMIT_BYTES = 56 * 1024 * 1024

TM_MIX0 = 512
TM_FFN = 256
TM_SSD = 256
CONF_ROWS = 32
CONF_HALO = 32
CONV_HALO = 8


def _const_spec(shape):
    nd = len(shape)
    return pl.BlockSpec(shape, lambda i: (0,) * nd, pipeline_mode=pl.Buffered(1))


def _row_spec(tm, width):
    return pl.BlockSpec((tm, width), lambda i: (i, 0))


def _gated(a, g):
    ha = 0.5 * a
    return ha + ha * jnp.tanh(0.5 * g)


def _silu(v):
    return _gated(v, v)


def _slab(s):
    return slice(s * LANES, (s + 1) * LANES)


def _rows2(start, n):
    return pl.ds(start, n, stride=2)


def _mod_norm(x, gain, scale, shift):
    ms = jnp.mean(x * x, axis=-1, keepdims=True)
    return (x * lax.rsqrt(ms + NORM_EPS)) * (gain * (1.0 + scale)) + shift


def _dot(a, b):
    return jnp.dot(a, b, preferred_element_type=F32)


def _mod_kernel(c_ref, w_ref, b_ref, o_ref):
    c = c_ref[...]
    s = _silu(c)
    o_ref[0] = jnp.sum(w_ref[0] * s, axis=0, keepdims=True) + b_ref[0]


def _modulation(c, w_mod, b_mod):
    depth, d, n = w_mod.shape
    nb = 1536
    assert n % nb == 0
    return pl.pallas_call(
        _mod_kernel,
        grid=(depth, n // nb),
        in_specs=[
            pl.BlockSpec((d, 1), lambda i, j: (0, 0)),
            pl.BlockSpec((1, d, nb), lambda i, j: (i, 0, j)),
            pl.BlockSpec((1, 1, nb), lambda i, j: (i, 0, j)),
        ],
        out_specs=pl.BlockSpec((1, 1, nb), lambda i, j: (i, 0, j)),
        out_shape=jax.ShapeDtypeStruct((depth, 1, n), F32),
        compiler_params=pltpu.CompilerParams(
            dimension_semantics=("arbitrary", "arbitrary"), vmem_limit_bytes=VMEM_LIMIT_BYTES),
        name="mod",
    )(c.reshape(d, 1), w_mod, b_mod.reshape(depth, 1, n))


def _mix0_kernel(x_ref, pos_ref, mod_ref, nm_ref, freq_ref, place_ref, win_ref, bin_ref, sink_ref, cw_ref, cb_ref,
                 lng_ref, lnb_ref, wout_ref, o_ref, kbuf, vbuf, uslab, hslab, merged, *, tm):
    i = pl.program_id(0)

    @pl.when(i == 0)
    def _():
        kbuf[:, 0:WINDOW, :] = jnp.zeros((4, WINDOW, LANES), BF16)
        vbuf[:, 0:WINDOW, :] = jnp.zeros((4, WINDOW, LANES), BF16)
        uslab[:, 0:CONF_HALO, :] = jnp.zeros((CONF_WIDTH // LANES, CONF_HALO, LANES), F32)

    x = x_ref[...]
    shift = mod_ref[:, 0:D_MODEL]
    scale = mod_ref[:, D_MODEL:2 * D_MODEL]
    gate = mod_ref[:, 2 * D_MODEL:3 * D_MODEL]
    h = _mod_norm(x, nm_ref[...], scale, shift).astype(BF16)
    proj = _dot(h, win_ref[...]) + bin_ref[...]

    ang = freq_ref[...] * pos_ref[...].astype(F32)
    frow = lax.broadcasted_iota(jnp.int32, ang.shape, 0)
    tab = jnp.where(frow < ROT_DIM // 2, jnp.cos(ang), jnp.sin(ang))
    tabs = None
    for part in _split3(tab):
        d = lax.dot_general(part, place_ref[...], (((0,), (0,)), ((), ())), preferred_element_type=F32)
        tabs = d if tabs is None else tabs + d
    lane = lax.broadcasted_iota(jnp.int32, (1, LANES), 1)
    cosv = tabs[:, 0:LANES] + jnp.where((lane & (HEAD_DIM - 1)) >= ROT_DIM, 1.0, 0.0)
    sin_up = tabs[:, LANES:2 * LANES]
    sin_dn = tabs[:, 2 * LANES:3 * LANES]
    lo = lane < HEAD_DIM

    def rope(t):
        return (t * cosv + pltpu.roll(t, ROT_DIM // 2, 1) * sin_up
                + pltpu.roll(t, LANES - ROT_DIM // 2, 1) * sin_dn)

    k = rope(proj[:, ATTN_WIDTH:ATTN_WIDTH + KV_WIDTH])
    v = proj[:, ATTN_WIDTH + KV_WIDTH:ATTN_WIDTH + 2 * KV_WIDTH]
    for buf, t in ((kbuf, k), (vbuf, v)):
        t_sw = pltpu.roll(t, HEAD_DIM, 1)
        buf[0, WINDOW:WINDOW + tm, :] = jnp.where(lo, t, 0.0).astype(BF16)
        buf[1, WINDOW:WINDOW + tm, :] = jnp.where(lo, 0.0, t_sw).astype(BF16)
        buf[2, WINDOW:WINDOW + tm, :] = jnp.where(lo, t_sw, 0.0).astype(BF16)
        buf[3, WINDOW:WINDOW + tm, :] = jnp.where(lo, 0.0, t).astype(BF16)

    q_slabs = [rope(proj[:, s * LANES:(s + 1) * LANES] * (HEAD_DIM ** -0.5)).astype(BF16)
               for s in range(ATTN_WIDTH // LANES)]

    row = lax.broadcasted_iota(jnp.int32, (2 * WINDOW, 1), 0)
    qi = row & (WINDOW - 1)
    kj = lax.broadcasted_iota(jnp.int32, (1, 2 * WINDOW), 1)
    band = (kj > qi) & (kj <= qi + WINDOW)
    band_first = band & ((kj >= WINDOW) | (i > 0))

    for b in range(tm // WINDOW):
        r0 = b * WINDOW
        mask = band_first if b == 0 else band
        for kh in range(N_KV_HEADS):
            qs = jnp.concatenate([q_slabs[2 * kh][r0:r0 + WINDOW], q_slabs[2 * kh + 1][r0:r0 + WINDOW]], axis=0)
            acc = None
            for par in range(2):
                sink = jnp.where(row < WINDOW, sink_ref[4 * kh + par], sink_ref[4 * kh + 2 + par])
                kk = kbuf[2 * kh + par, r0:r0 + 2 * WINDOW, :]
                s = lax.dot_general(qs, kk, (((1,), (1,)), ((), ())), preferred_element_type=F32)
                s = jnp.where(mask, s, -jnp.inf)
                m = jnp.maximum(jnp.max(s, axis=-1, keepdims=True), sink)
                p = jnp.exp(s - m)
                denom = jnp.sum(p, axis=-1, keepdims=True) + jnp.exp(sink - m)
                o = _dot(p.astype(BF16), vbuf[2 * kh + par, r0:r0 + 2 * WINDOW, :]) * (1.0 / denom)
                acc = o if acc is None else acc + o
            merged[r0:r0 + WINDOW, 2 * kh * LANES:(2 * kh + 1) * LANES] = acc[0:WINDOW].astype(BF16)
            merged[r0:r0 + WINDOW, (2 * kh + 1) * LANES:(2 * kh + 2) * LANES] = acc[WINDOW:].astype(BF16)

    kbuf[:, 0:WINDOW, :] = kbuf[:, tm:tm + WINDOW, :]
    vbuf[:, 0:WINDOW, :] = vbuf[:, tm:tm + WINDOW, :]

    c0 = ATTN_WIDTH + 2 * KV_WIDTH
    n_cs = CONF_WIDTH // LANES
    for s in range(n_cs):
        uslab[s, CONF_HALO:CONF_HALO + tm, :] = _gated(proj[:, c0 + s * LANES:c0 + (s + 1) * LANES],
                                                      proj[:, c0 + CONF_WIDTH + s * LANES:
                                                           c0 + CONF_WIDTH + (s + 1) * LANES])
    tap0 = CONF_HALO - (CONF_KERNEL - 1)
    half = CONF_ROWS // 2

    def conf_chunk(ci, carry):
        r0 = pl.multiple_of(ci * CONF_ROWS, CONF_ROWS)
        for par in range(2):
            accs = []
            for s in range(n_cs):
                acc = jnp.zeros((half, LANES), F32) + cb_ref[:, _slab(s)]
                for kk in range(CONF_KERNEL):
                    acc = acc + cw_ref[kk:kk + 1, _slab(s)] * uslab[s, _rows2(r0 + tap0 + kk + par, half), :]
                accs.append(acc)
            mu = sum(jnp.sum(a, axis=-1, keepdims=True) for a in accs) * (1.0 / CONF_WIDTH)
            cens = [a - mu for a in accs]
            var = sum(jnp.sum(cn * cn, axis=-1, keepdims=True) for cn in cens) * (1.0 / CONF_WIDTH)
            rstd = lax.rsqrt(var + LN_EPS)
            for s in range(n_cs):
                hn = cens[s] * rstd * lng_ref[:, _slab(s)] + lnb_ref[:, _slab(s)]
                hslab[s, _rows2(r0 + par, half), :] = _silu(hn)
        return carry

    lax.fori_loop(0, tm // CONF_ROWS, conf_chunk, 0, unroll=4)
    uslab[:, 0:CONF_HALO, :] = uslab[:, tm:tm + CONF_HALO, :]
    for s in range(n_cs):
        merged[:, ATTN_WIDTH + s * LANES:ATTN_WIDTH + (s + 1) * LANES] = hslab[s].astype(BF16)

    y = _dot(merged[...], wout_ref[...])
    o_ref[...] = x + gate * y


def _mix0(x, pos, mod, norm_g, freq, place, w_in, b_in, sinks, conv_w, conv_b, ln_g, ln_b, w_out):
    s = x.shape[0]
    tm = TM_MIX0
    assert s % tm == 0 and tm % WINDOW == 0 and tm % CONF_ROWS == 0
    return pl.pallas_call(
        functools.partial(_mix0_kernel, tm=tm),
        grid=(s // tm,),
        in_specs=[
            _row_spec(tm, D_MODEL),
            pl.BlockSpec((1, tm), lambda i: (0, i)),
            _const_spec(mod.shape),
            _const_spec(norm_g.shape),
            _const_spec(freq.shape),
            _const_spec(place.shape),
            _const_spec(w_in.shape),
            _const_spec(b_in.shape),
            pl.BlockSpec(memory_space=pltpu.SMEM),
            _const_spec(conv_w.shape),
            _const_spec(conv_b.shape),
            _const_spec(ln_g.shape),
            _const_spec(ln_b.shape),
            _const_spec(w_out.shape),
        ],
        out_specs=_row_spec(tm, D_MODEL),
        out_shape=jax.ShapeDtypeStruct((s, D_MODEL), F32),
        scratch_shapes=[
            pltpu.VMEM((4, WINDOW + tm, LANES), BF16),
            pltpu.VMEM((4, WINDOW + tm, LANES), BF16),
            pltpu.VMEM((CONF_WIDTH // LANES, CONF_HALO + tm, LANES), F32),
            pltpu.VMEM((CONF_WIDTH // LANES, tm, LANES), F32),
            pltpu.VMEM((tm, D_MODEL), BF16),
        ],
        compiler_params=pltpu.CompilerParams(
            dimension_semantics=("arbitrary",), vmem_limit_bytes=VMEM_LIMIT_BYTES),
        name="mix0",
    )(x, pos, mod, norm_g, freq, place, w_in, b_in, sinks, conv_w, conv_b, ln_g, ln_b, w_out)


def _ffn_kernel(x_ref, mod_ref, nf_ref, wg_ref, wv_ref, cw_ref, cb_ref, wd_ref, fn_ref, o_ref, gext,
                *, tm, final_norm):
    i = pl.program_id(0)

    @pl.when(i == 0)
    def _():
        gext[0:CONV_HALO, :] = jnp.zeros((CONV_HALO, D_FF), F32)

    x = x_ref[...]
    shift = mod_ref[:, 3 * D_MODEL:4 * D_MODEL]
    scale = mod_ref[:, 4 * D_MODEL:5 * D_MODEL]
    gate = mod_ref[:, 5 * D_MODEL:6 * D_MODEL]
    h = _mod_norm(x, nf_ref[...], scale, shift).astype(BF16)
    gext[CONV_HALO:CONV_HALO + tm, :] = _dot(h, wg_ref[...])
    val = _dot(h, wv_ref[...])
    g = cb_ref[...]
    for kk in range(FFN_CONV):
        r = CONV_HALO - (FFN_CONV - 1) + kk
        g = g + cw_ref[kk:kk + 1, :] * gext[r:r + tm, :]
    gext[0:CONV_HALO, :] = gext[tm:tm + CONV_HALO, :]
    act = (_silu(g) * val).astype(BF16)
    y = x + gate * _dot(act, wd_ref[...])
    if final_norm:
        ms = jnp.mean(y * y, axis=-1, keepdims=True)
        y = y * lax.rsqrt(ms + NORM_EPS) * fn_ref[...]
    o_ref[...] = y


def _ffn(x, mod, norm_g, w_gate, conv_w, conv_b, w_val, w_down, final_g, final_norm):
    s = x.shape[0]
    tm = TM_FFN
    assert s % tm == 0
    return pl.pallas_call(
        functools.partial(_ffn_kernel, tm=tm, final_norm=final_norm),
        grid=(s // tm,),
        in_specs=[
            _row_spec(tm, D_MODEL),
            _const_spec(mod.shape),
            _const_spec(norm_g.shape),
            _const_spec(w_gate.shape),
            _const_spec(w_val.shape),
            _const_spec(conv_w.shape),
            _const_spec(conv_b.shape),
            _const_spec(w_down.shape),
            _const_spec(final_g.shape),
        ],
        out_specs=_row_spec(tm, D_MODEL),
        out_shape=jax.ShapeDtypeStruct((s, D_MODEL), F32),
        scratch_shapes=[pltpu.VMEM((CONV_HALO + tm, D_FF), F32)],
        compiler_params=pltpu.CompilerParams(
            dimension_semantics=("arbitrary",), vmem_limit_bytes=VMEM_LIMIT_BYTES),
        name="ffn_final" if final_norm else "ffn",
    )(x, mod, norm_g, w_gate, w_val, conv_w, conv_b, w_down, final_g)


def _split3(v):
    hi = v.astype(BF16)
    r1 = v - hi.astype(F32)
    mid = r1.astype(BF16)
    lo = (r1 - mid.astype(F32)).astype(BF16)
    return hi, mid, lo


def _ssd_kernel(x_ref, mod_ref, nm_ref, wz_ref, wx_ref, wdt_ref, cw_ref, cb_ref, dtb_ref, alog_ref, dskip_ref,
                ng_ref, wo_ref, o_ref, xslab, xact, dts, adts, state, yscr, *, tm):
    i = pl.program_id(0)

    @pl.when(i == 0)
    def _():
        xslab[:, 0:CONV_HALO, :] = jnp.zeros((XBC_WIDTH // LANES, CONV_HALO, LANES), F32)
        state[...] = jnp.zeros((D_STATE, D_INNER), F32)

    x = x_ref[...]
    shift = mod_ref[:, 0:D_MODEL]
    scale = mod_ref[:, D_MODEL:2 * D_MODEL]
    gate = mod_ref[:, 2 * D_MODEL:3 * D_MODEL]
    h = _mod_norm(x, nm_ref[...], scale, shift).astype(BF16)

    n_s = XBC_WIDTH // LANES
    xraw = _dot(h, wx_ref[...])
    for s in range(n_s):
        xslab[s, CONV_HALO:CONV_HALO + tm, :] = xraw[:, _slab(s)]
    tap0 = CONV_HALO - (SSM_CONV - 1)
    for s in range(n_s):
        for par in range(2):
            acc = cb_ref[:, _slab(s)]
            for kk in range(SSM_CONV):
                acc = acc + cw_ref[kk:kk + 1, _slab(s)] * xslab[s, _rows2(tap0 + kk + par, tm // 2), :]
            xact[s, _rows2(par, tm // 2), :] = _silu(acc)
    xslab[:, 0:CONV_HALO, :] = xslab[:, tm:tm + CONV_HALO, :]

    dt_in = _dot(h, wdt_ref[...]) + dtb_ref[...]
    dt = jnp.maximum(dt_in, 0.0) + jnp.log1p(jnp.exp(-jnp.abs(dt_in)))
    dts[...] = dt
    adts[...] = dt * (-jnp.exp(alog_ref[...]))

    ri = lax.broadcasted_iota(jnp.int32, (CHUNK, CHUNK), 0)
    ci = lax.broadcasted_iota(jnp.int32, (CHUNK, CHUNK), 1)
    causal = ri >= ci
    tril = jnp.where(causal, 1.0, 0.0).astype(BF16)
    lo = lax.broadcasted_iota(jnp.int32, (1, LANES), 1) < SSM_HEADDIM
    b0 = D_INNER // LANES
    c0 = b0 + SSM_GROUPS * D_STATE // LANES

    def chunk_body(c, carry):
        r = pl.multiple_of(c * CHUNK, CHUNK)
        dt_t = dts[pl.ds(r, CHUNK), :].T
        hi, mid, low = _split3(adts[pl.ds(r, CHUNK), :])
        a_cs = _dot(tril, hi) + _dot(tril, mid) + _dot(tril, low)
        a_cs_t = a_cs.T
        w_end_t = dt_t * jnp.exp(a_cs_t[:, CHUNK - 1:CHUNK] - a_cs_t)
        cdec = jnp.exp(a_cs[CHUNK - 1:CHUNK, :])
        for g in range(SSM_GROUPS):
            bm_t = xact[b0 + g, pl.ds(r, CHUNK), :].T
            cm_g = xact[c0 + g, pl.ds(r, CHUNK), :]
            cb = _dot(cm_g.astype(BF16), bm_t.astype(BF16))
            for jp in range(HEADS_PER_GROUP // 2):
                sl = g * (HEADS_PER_GROUP // 2) + jp
                xs_p = xact[sl, pl.ds(r, CHUNK), :]
                st_p = state[:, sl * LANES:(sl + 1) * LANES]
                y_p = None
                s_new = None
                for par in range(2):
                    hd = 2 * sl + par
                    keep = lo if par == 0 else jnp.logical_not(lo)
                    xs_m = jnp.where(keep, xs_p, 0.0).astype(BF16)
                    st_m = jnp.where(keep, st_p, 0.0).astype(BF16)
                    a_l = jnp.broadcast_to(a_cs[:, hd:hd + 1], (CHUNK, CHUNK))
                    a_s = a_cs_t[hd:hd + 1, :]
                    decay = jnp.exp(jnp.where(causal, a_l - a_s, -jnp.inf))
                    m_in = cb * decay * dt_t[hd:hd + 1, :]
                    c_off = cm_g * jnp.exp(a_l)
                    lhs = jnp.concatenate([m_in, c_off], axis=1).astype(BF16)
                    rhs = jnp.concatenate([xs_m, st_m], axis=0)
                    y_h = _dot(lhs, rhs)
                    b_dec_t = (bm_t * w_end_t[hd:hd + 1, :]).astype(BF16)
                    s_h = _dot(b_dec_t, xs_m)
                    y_p = y_h if y_p is None else y_p + y_h
                    s_new = s_h if s_new is None else s_new + s_h
                cd_p = jnp.where(lo, cdec[:, 2 * sl:2 * sl + 1], cdec[:, 2 * sl + 1:2 * sl + 2])
                state[:, sl * LANES:(sl + 1) * LANES] = cd_p * st_p + s_new
                yscr[pl.ds(r, CHUNK), sl * LANES:(sl + 1) * LANES] = (
                    y_p + xs_p * dskip_ref[:, sl * LANES:(sl + 1) * LANES])
        return carry

    lax.fori_loop(0, tm // CHUNK, chunk_body, 0, unroll=True)

    z = _dot(h, wz_ref[...])
    y = yscr[...] * _silu(z)
    gw = D_INNER // SSM_GROUPS
    parts = []
    for g in range(SSM_GROUPS):
        yg = y[:, g * gw:(g + 1) * gw]
        ms = jnp.mean(yg * yg, axis=-1, keepdims=True)
        parts.append((yg * lax.rsqrt(ms + NORM_EPS) * ng_ref[:, g * gw:(g + 1) * gw]).astype(BF16))
    yn = jnp.concatenate(parts, axis=1)
    o_ref[...] = x + gate * _dot(yn, wo_ref[...])


def _ssd(x, mod, norm_g, w_z, w_x, w_dt, conv_w, conv_b, dt_bias, a_log, d_skip, norm_y, w_out):
    s = x.shape[0]
    tm = TM_SSD
    assert s % tm == 0 and tm % CHUNK == 0
    return pl.pallas_call(
        functools.partial(_ssd_kernel, tm=tm),
        grid=(s // tm,),
        in_specs=[
            _row_spec(tm, D_MODEL),
            _const_spec(mod.shape),
            _const_spec(norm_g.shape),
            _const_spec(w_z.shape),
            _const_spec(w_x.shape),
            _const_spec(w_dt.shape),
            _const_spec(conv_w.shape),
            _const_spec(conv_b.shape),
            _const_spec(dt_bias.shape),
            _const_spec(a_log.shape),
            _const_spec(d_skip.shape),
            _const_spec(norm_y.shape),
            _const_spec(w_out.shape),
        ],
        out_specs=_row_spec(tm, D_MODEL),
        out_shape=jax.ShapeDtypeStruct((s, D_MODEL), F32),
        scratch_shapes=[
            pltpu.VMEM((XBC_WIDTH // LANES, CONV_HALO + tm, LANES), F32),
            pltpu.VMEM((XBC_WIDTH // LANES, tm, LANES), F32),
            pltpu.VMEM((tm, LANES), F32),
            pltpu.VMEM((tm, LANES), F32),
            pltpu.VMEM((D_STATE, D_INNER), F32),
            pltpu.VMEM((tm, D_INNER), F32),
        ],
        compiler_params=pltpu.CompilerParams(
            dimension_semantics=("arbitrary",), vmem_limit_bytes=VMEM_LIMIT_BYTES),
        name="ssd",
    )(x, mod, norm_g, w_z, w_x, w_dt, conv_w, conv_b, dt_bias, a_log, d_skip, norm_y, w_out)


def _pad_lanes(v, width=LANES):
    return jnp.pad(v, ((0, 0), (0, width - v.shape[1])))


def kernel(x, c, positions, w_mod, b_mod, norm_mix, norm_ffn, w_in_e, b_in_e, attn_sinks, conf_conv_w,
           conf_conv_b, conf_ln_g, conf_ln_b, w_out_e, w_in_o, ssm_conv_w, ssm_conv_b, ssm_dt_bias, ssm_a_log,
           ssm_d, ssm_norm_g, w_out_o, ffn_w_gate, ffn_conv_w, ffn_conv_b, ffn_w_val, ffn_w_down, final_norm):
    b, s, d = x.shape
    assert b == 1 and d == D_MODEL and w_mod.shape[0] == 2
    xs = x.reshape(s, d)
    pos = positions.reshape(1, s)

    mod = _modulation(c, w_mod, b_mod)

    half = ROT_DIM // 2
    inv_freq = 1.0 / (ROPE_THETA ** (jnp.arange(half, dtype=F32) * 2.0 / ROT_DIM))
    freq = jnp.concatenate([inv_freq, inv_freq]).reshape(ROT_DIM, 1)
    hl = np.arange(LANES) % HEAD_DIM
    fr = np.arange(ROT_DIM)[:, None]
    is_cos = fr < half
    place = np.concatenate([
        np.where(is_cos & (hl[None, :] < ROT_DIM) & (hl[None, :] % half == fr), 1.0, 0.0),
        np.where(~is_cos & (hl[None, :] >= half) & (hl[None, :] < ROT_DIM) & (hl[None, :] % half == fr - half), 1.0, 0.0),
        np.where(~is_cos & (hl[None, :] < half) & (hl[None, :] == fr - half), -1.0, 0.0)], axis=1)
    place = jnp.asarray(place, BF16)

    row = lambda v: v.reshape(1, -1)
    final_g = row(final_norm)

    xs = _mix0(xs, pos, mod[0], row(norm_mix[0]), freq, place, w_in_e[0].astype(BF16), row(b_in_e[0]), attn_sinks[0],
               jnp.pad(conf_conv_w[0], ((0, 1), (0, 0))), row(conf_conv_b[0]), row(conf_ln_g[0]),
               row(conf_ln_b[0]), w_out_e[0].astype(BF16))
    xs = _ffn(xs, mod[0], row(norm_ffn[0]), ffn_w_gate[0].astype(BF16), ffn_conv_w[0], row(ffn_conv_b[0]),
              ffn_w_val[0].astype(BF16), ffn_w_down[0].astype(BF16), final_g, False)

    w_in = w_in_o[0]
    w_z = w_in[:, :D_INNER].astype(BF16)
    w_x = w_in[:, D_INNER:D_INNER + XBC_WIDTH].astype(BF16)
    w_dt = _pad_lanes(w_in[:, D_INNER + XBC_WIDTH:]).astype(BF16)
    d_skip = jnp.repeat(ssm_d[0], SSM_HEADDIM).reshape(1, D_INNER)
    xs = _ssd(xs, mod[1], row(norm_mix[1]), w_z, w_x, w_dt, ssm_conv_w[0], row(ssm_conv_b[0]),
              _pad_lanes(row(ssm_dt_bias[0])), _pad_lanes(row(ssm_a_log[0])), d_skip, row(ssm_norm_g[0]),
              w_out_o[0].astype(BF16))
    xs = _ffn(xs, mod[1], row(norm_ffn[1]), ffn_w_gate[1].astype(BF16), ffn_conv_w[1], row(ffn_conv_b[1]),
              ffn_w_val[1].astype(BF16), ffn_w_down[1].astype(BF16), final_g, True)
    return xs.reshape(b, s, d)
```

```python
import functools

import jax
import jax.numpy as jnp
import numpy as np
from jax import lax
from jax.experimental import pallas as pl
from jax.experimental.pallas import tpu as pltpu

F32 = jnp.float32
BF16 = jnp.bfloat16

D_MODEL = 1024
N_Q_HEADS = 8
N_KV_HEADS = 2
HEAD_DIM = 64
ATTN_WIDTH = N_Q_HEADS * HEAD_DIM
KV_WIDTH = N_KV_HEADS * HEAD_DIM
WINDOW = 128
ROT_DIM = HEAD_DIM // 4
ROPE_THETA = 500000.0
CONF_WIDTH = D_MODEL // 2
CONF_KERNEL = 31
E_IN_COLS = ATTN_WIDTH + 2 * KV_WIDTH + 2 * CONF_WIDTH
D_INNER = 2 * D_MODEL
SSM_HEADDIM = 64
SSM_HEADS = D_INNER // SSM_HEADDIM
SSM_GROUPS = 4
HEADS_PER_GROUP = SSM_HEADS // SSM_GROUPS
D_STATE = 128
SSM_CONV = 4
CHUNK = 128
XBC_WIDTH = D_INNER + 2 * SSM_GROUPS * D_STATE
D_FF = 2816
FFN_CONV = 3
NORM_EPS = 1e-6
LN_EPS = 1e-5

LANES = 128
SUBLANES = 8
VMEM_LIMIT_BYTES = 56 * 1024 * 1024

TM_MIX0 = 512
TM_FFN = 256
TM_SSD = 256
CONF_ROWS = 32
CONF_HALO = 32
CONV_HALO = 8


def _const_spec(shape):
    nd = len(shape)
    return pl.BlockSpec(shape, lambda i: (0,) * nd, pipeline_mode=pl.Buffered(1))


def _layer_spec(shape, layer):
    return pl.BlockSpec((1,) + tuple(shape[1:]), lambda i: (layer, 0, 0), pipeline_mode=pl.Buffered(1))


def _row_spec(tm, width):
    return pl.BlockSpec((tm, width), lambda i: (i, 0))


def _gated(a, g):
    ha = 0.5 * a
    return ha + ha * jnp.tanh(0.5 * g)


def _silu(v):
    return _gated(v, v)


def _slab(s):
    return slice(s * LANES, (s + 1) * LANES)


def _rows2(start, n):
    return pl.ds(start, n, stride=2)


def _mod_norm(x, gain, scale, shift):
    ms = jnp.mean(x * x, axis=-1, keepdims=True)
    return (x * lax.rsqrt(ms + NORM_EPS)) * (gain * (1.0 + scale)) + shift


def _dot(a, b):
    return jnp.dot(a, b, preferred_element_type=F32)


def _mod_kernel(c_ref, w_ref, b_ref, o_ref):
    c = c_ref[...]
    s = _silu(c)
    o_ref[0] = jnp.sum(w_ref[0] * s, axis=0, keepdims=True) + b_ref[0]


def _modulation(c, w_mod, b_mod):
    depth, d, n = w_mod.shape
    nb = 1536
    assert n % nb == 0
    return pl.pallas_call(
        _mod_kernel,
        grid=(depth, n // nb),
        in_specs=[
            pl.BlockSpec((d, 1), lambda i, j: (0, 0)),
            pl.BlockSpec((1, d, nb), lambda i, j: (i, 0, j)),
            pl.BlockSpec((1, 1, nb), lambda i, j: (i, 0, j)),
        ],
        out_specs=pl.BlockSpec((1, 1, nb), lambda i, j: (i, 0, j)),
        out_shape=jax.ShapeDtypeStruct((depth, 1, n), F32),
        compiler_params=pltpu.CompilerParams(
            dimension_semantics=("arbitrary", "arbitrary"), vmem_limit_bytes=VMEM_LIMIT_BYTES),
        name="mod",
    )(c.reshape(d, 1), w_mod, b_mod.reshape(depth, 1, n))


def _mix0_kernel(x_ref, pos_ref, mod_ref, nm_ref, freq_ref, place_ref, win_ref, bin_ref, sink_ref, cw_ref, cb_ref,
                 lng_ref, lnb_ref, wout_ref, o_ref, kbuf, vbuf, uslab, hslab, merged, *, tm):
    i = pl.program_id(0)

    @pl.when(i == 0)
    def _():
        kbuf[:, 0:WINDOW, :] = jnp.zeros((4, WINDOW, LANES), BF16)
        vbuf[:, 0:WINDOW, :] = jnp.zeros((4, WINDOW, LANES), BF16)
        uslab[:, 0:CONF_HALO, :] = jnp.zeros((CONF_WIDTH // LANES, CONF_HALO, LANES), F32)

    x = x_ref[...]
    shift = mod_ref[:, 0:D_MODEL]
    scale = mod_ref[:, D_MODEL:2 * D_MODEL]
    gate = mod_ref[:, 2 * D_MODEL:3 * D_MODEL]
    h = _mod_norm(x, nm_ref[...], scale, shift).astype(BF16)
    proj = _dot(h, win_ref[...]) + bin_ref[...]

    ang = freq_ref[...] * pos_ref[...].astype(F32)
    frow = lax.broadcasted_iota(jnp.int32, ang.shape, 0)
    tab = jnp.where(frow < ROT_DIM // 2, jnp.cos(ang), jnp.sin(ang))
    tabs = None
    for part in _split3(tab):
        d = lax.dot_general(part, place_ref[...], (((0,), (0,)), ((), ())), preferred_element_type=F32)
        tabs = d if tabs is None else tabs + d
    lane = lax.broadcasted_iota(jnp.int32, (1, LANES), 1)
    cosv = tabs[:, 0:LANES] + jnp.where((lane & (HEAD_DIM - 1)) >= ROT_DIM, 1.0, 0.0)
    sin_up = tabs[:, LANES:2 * LANES]
    sin_dn = tabs[:, 2 * LANES:3 * LANES]
    lo = lane < HEAD_DIM

    def rope(t):
        return (t * cosv + pltpu.roll(t, ROT_DIM // 2, 1) * sin_up
                + pltpu.roll(t, LANES - ROT_DIM // 2, 1) * sin_dn)

    k = rope(proj[:, ATTN_WIDTH:ATTN_WIDTH + KV_WIDTH])
    v = proj[:, ATTN_WIDTH + KV_WIDTH:ATTN_WIDTH + 2 * KV_WIDTH]
    for buf, t in ((kbuf, k), (vbuf, v)):
        t_sw = pltpu.roll(t, HEAD_DIM, 1)
        buf[0, WINDOW:WINDOW + tm, :] = jnp.where(lo, t, 0.0).astype(BF16)
        buf[1, WINDOW:WINDOW + tm, :] = jnp.where(lo, 0.0, t_sw).astype(BF16)
        buf[2, WINDOW:WINDOW + tm, :] = jnp.where(lo, t_sw, 0.0).astype(BF16)
        buf[3, WINDOW:WINDOW + tm, :] = jnp.where(lo, 0.0, t).astype(BF16)

    q_slabs = [rope(proj[:, s * LANES:(s + 1) * LANES] * (HEAD_DIM ** -0.5)).astype(BF16)
               for s in range(ATTN_WIDTH // LANES)]

    row = lax.broadcasted_iota(jnp.int32, (2 * WINDOW, 1), 0)
    qi = row & (WINDOW - 1)
    kj = lax.broadcasted_iota(jnp.int32, (1, 2 * WINDOW), 1)
    band = (kj > qi) & (kj <= qi + WINDOW)
    band_first = band & ((kj >= WINDOW) | (i > 0))

    for b in range(tm // WINDOW):
        r0 = b * WINDOW
        mask = band_first if b == 0 else band
        for kh in range(N_KV_HEADS):
            qs = jnp.concatenate([q_slabs[2 * kh][r0:r0 + WINDOW], q_slabs[2 * kh + 1][r0:r0 + WINDOW]], axis=0)
            acc = None
            for par in range(2):
                sink = jnp.where(row < WINDOW, sink_ref[4 * kh + par], sink_ref[4 * kh + 2 + par])
                kk = kbuf[2 * kh + par, r0:r0 + 2 * WINDOW, :]
                s = lax.dot_general(qs, kk, (((1,), (1,)), ((), ())), preferred_element_type=F32)
                s = jnp.where(mask, s, -jnp.inf)
                m = jnp.maximum(jnp.max(s, axis=-1, keepdims=True), sink)
                p = jnp.exp(s - m)
                denom = jnp.sum(p, axis=-1, keepdims=True) + jnp.exp(sink - m)
                o = _dot(p.astype(BF16), vbuf[2 * kh + par, r0:r0 + 2 * WINDOW, :]) * (1.0 / denom)
                acc = o if acc is None else acc + o
            merged[r0:r0 + WINDOW, 2 * kh * LANES:(2 * kh + 1) * LANES] = acc[0:WINDOW].astype(BF16)
            merged[r0:r0 + WINDOW, (2 * kh + 1) * LANES:(2 * kh + 2) * LANES] = acc[WINDOW:].astype(BF16)

    kbuf[:, 0:WINDOW, :] = kbuf[:, tm:tm + WINDOW, :]
    vbuf[:, 0:WINDOW, :] = vbuf[:, tm:tm + WINDOW, :]

    c0 = ATTN_WIDTH + 2 * KV_WIDTH
    n_cs = CONF_WIDTH // LANES
    for s in range(n_cs):
        uslab[s, CONF_HALO:CONF_HALO + tm, :] = _gated(proj[:, c0 + s * LANES:c0 + (s + 1) * LANES],
                                                      proj[:, c0 + CONF_WIDTH + s * LANES:
                                                           c0 + CONF_WIDTH + (s + 1) * LANES])
    tap0 = CONF_HALO - (CONF_KERNEL - 1)
    half = CONF_ROWS // 2

    def conf_chunk(ci, carry):
        r0 = pl.multiple_of(ci * CONF_ROWS, CONF_ROWS)
        for par in range(2):
            accs = []
            for s in range(n_cs):
                acc = jnp.zeros((half, LANES), F32) + cb_ref[:, _slab(s)]
                for kk in range(CONF_KERNEL):
                    acc = acc + cw_ref[kk:kk + 1, _slab(s)] * uslab[s, _rows2(r0 + tap0 + kk + par, half), :]
                accs.append(acc)
            mu = sum(jnp.sum(a, axis=-1, keepdims=True) for a in accs) * (1.0 / CONF_WIDTH)
            cens = [a - mu for a in accs]
            var = sum(jnp.sum(cn * cn, axis=-1, keepdims=True) for cn in cens) * (1.0 / CONF_WIDTH)
            rstd = lax.rsqrt(var + LN_EPS)
            for s in range(n_cs):
                hn = cens[s] * rstd * lng_ref[:, _slab(s)] + lnb_ref[:, _slab(s)]
                hslab[s, _rows2(r0 + par, half), :] = _silu(hn)
        return carry

    lax.fori_loop(0, tm // CONF_ROWS, conf_chunk, 0, unroll=4)
    uslab[:, 0:CONF_HALO, :] = uslab[:, tm:tm + CONF_HALO, :]
    for s in range(n_cs):
        merged[:, ATTN_WIDTH + s * LANES:ATTN_WIDTH + (s + 1) * LANES] = hslab[s].astype(BF16)

    y = _dot(merged[...], wout_ref[...])
    o_ref[...] = x + gate * y


def _mix0(x, pos, mod, norm_g, freq, place, w_in, b_in, sinks, conv_w, conv_b, ln_g, ln_b, w_out):
    s = x.shape[0]
    tm = TM_MIX0
    assert s % tm == 0 and tm % WINDOW == 0 and tm % CONF_ROWS == 0
    return pl.pallas_call(
        functools.partial(_mix0_kernel, tm=tm),
        grid=(s // tm,),
        in_specs=[
            _row_spec(tm, D_MODEL),
            pl.BlockSpec((1, tm), lambda i: (0, i)),
            _const_spec(mod.shape),
            _const_spec(norm_g.shape),
            _const_spec(freq.shape),
            _const_spec(place.shape),
            _const_spec(w_in.shape),
            _const_spec(b_in.shape),
            pl.BlockSpec(memory_space=pltpu.SMEM),
            _const_spec(conv_w.shape),
            _const_spec(conv_b.shape),
            _const_spec(ln_g.shape),
            _const_spec(ln_b.shape),
            _const_spec(w_out.shape),
        ],
        out_specs=_row_spec(tm, D_MODEL),
        out_shape=jax.ShapeDtypeStruct((s, D_MODEL), F32),
        scratch_shapes=[
            pltpu.VMEM((4, WINDOW + tm, LANES), BF16),
            pltpu.VMEM((4, WINDOW + tm, LANES), BF16),
            pltpu.VMEM((CONF_WIDTH // LANES, CONF_HALO + tm, LANES), F32),
            pltpu.VMEM((CONF_WIDTH // LANES, tm, LANES), F32),
            pltpu.VMEM((tm, D_MODEL), BF16),
        ],
        compiler_params=pltpu.CompilerParams(
            dimension_semantics=("arbitrary",), vmem_limit_bytes=VMEM_LIMIT_BYTES),
        name="mix0",
    )(x, pos, mod, norm_g, freq, place, w_in, b_in, sinks, conv_w, conv_b, ln_g, ln_b, w_out)


def _ffn_kernel(x_ref, mod_ref, nf_ref, wg_ref, wv_ref, cw_ref, cb_ref, wd_ref, fn_ref, o_ref, gext,
                *, tm, final_norm):
    i = pl.program_id(0)

    @pl.when(i == 0)
    def _():
        gext[0:CONV_HALO, :] = jnp.zeros((CONV_HALO, D_FF), F32)

    x = x_ref[...]
    shift = mod_ref[:, 3 * D_MODEL:4 * D_MODEL]
    scale = mod_ref[:, 4 * D_MODEL:5 * D_MODEL]
    gate = mod_ref[:, 5 * D_MODEL:6 * D_MODEL]
    h = _mod_norm(x, nf_ref[...], scale, shift).astype(BF16)
    gext[CONV_HALO:CONV_HALO + tm, :] = _dot(h, wg_ref[0])
    val = _dot(h, wv_ref[0])
    g = cb_ref[...]
    for kk in range(FFN_CONV):
        r = CONV_HALO - (FFN_CONV - 1) + kk
        g = g + cw_ref[kk:kk + 1, :] * gext[r:r + tm, :]
    gext[0:CONV_HALO, :] = gext[tm:tm + CONV_HALO, :]
    act = (_silu(g) * val).astype(BF16)
    y = x + gate * _dot(act, wd_ref[0])
    if final_norm:
        ms = jnp.mean(y * y, axis=-1, keepdims=True)
        y = y * lax.rsqrt(ms + NORM_EPS) * fn_ref[...]
    o_ref[...] = y


def _ffn(x, mod, norm_g, w_gate, conv_w, conv_b, w_val, w_down, final_g, layer, final_norm):
    s = x.shape[0]
    tm = TM_FFN
    assert s % tm == 0
    return pl.pallas_call(
        functools.partial(_ffn_kernel, tm=tm, final_norm=final_norm),
        grid=(s // tm,),
        in_specs=[
            _row_spec(tm, D_MODEL),
            _const_spec(mod.shape),
            _const_spec(norm_g.shape),
            _layer_spec(w_gate.shape, layer),
            _layer_spec(w_val.shape, layer),
            _const_spec(conv_w.shape),
            _const_spec(conv_b.shape),
            _layer_spec(w_down.shape, layer),
            _const_spec(final_g.shape),
        ],
        out_specs=_row_spec(tm, D_MODEL),
        out_shape=jax.ShapeDtypeStruct((s, D_MODEL), F32),
        scratch_shapes=[pltpu.VMEM((CONV_HALO + tm, D_FF), F32)],
        compiler_params=pltpu.CompilerParams(
            dimension_semantics=("arbitrary",), vmem_limit_bytes=VMEM_LIMIT_BYTES),
        name="ffn_final" if final_norm else "ffn",
    )(x, mod, norm_g, w_gate, w_val, conv_w, conv_b, w_down, final_g)


def _split3(v):
    hi = v.astype(BF16)
    r1 = v - hi.astype(F32)
    mid = r1.astype(BF16)
    lo = (r1 - mid.astype(F32)).astype(BF16)
    return hi, mid, lo


def _ssd_kernel(x_ref, mod_ref, nm_ref, win_ref, wdt_ref, cw_ref, cb_ref, dtb_ref, alog_ref, dskip_ref,
                ng_ref, wo_ref, o_ref, xslab, xact, dts, adts, state, yscr, *, tm):
    i = pl.program_id(0)

    @pl.when(i == 0)
    def _():
        xslab[:, 0:CONV_HALO, :] = jnp.zeros((XBC_WIDTH // LANES, CONV_HALO, LANES), F32)
        state[...] = jnp.zeros((D_STATE, D_INNER), F32)

    x = x_ref[...]
    shift = mod_ref[:, 0:D_MODEL]
    scale = mod_ref[:, D_MODEL:2 * D_MODEL]
    gate = mod_ref[:, 2 * D_MODEL:3 * D_MODEL]
    h = _mod_norm(x, nm_ref[...], scale, shift).astype(BF16)

    n_s = XBC_WIDTH // LANES
    xraw = _dot(h, win_ref[0, :, D_INNER:D_INNER + XBC_WIDTH])
    for s in range(n_s):
        xslab[s, CONV_HALO:CONV_HALO + tm, :] = xraw[:, _slab(s)]
    tap0 = CONV_HALO - (SSM_CONV - 1)
    for s in range(n_s):
        for par in range(2):
            acc = cb_ref[:, _slab(s)]
            for kk in range(SSM_CONV):
                acc = acc + cw_ref[kk:kk + 1, _slab(s)] * xslab[s, _rows2(tap0 + kk + par, tm // 2), :]
            xact[s, _rows2(par, tm // 2), :] = _silu(acc)
    xslab[:, 0:CONV_HALO, :] = xslab[:, tm:tm + CONV_HALO, :]

    dt_in = _dot(h, wdt_ref[...]) + dtb_ref[...]
    dt = jnp.maximum(dt_in, 0.0) + jnp.log1p(jnp.exp(-jnp.abs(dt_in)))
    dts[...] = dt
    adts[...] = dt * (-jnp.exp(alog_ref[...]))

    ri = lax.broadcasted_iota(jnp.int32, (CHUNK, CHUNK), 0)
    ci = lax.broadcasted_iota(jnp.int32, (CHUNK, CHUNK), 1)
    causal = ri >= ci
    tril = jnp.where(causal, 1.0, 0.0).astype(BF16)
    lo = lax.broadcasted_iota(jnp.int32, (1, LANES), 1) < SSM_HEADDIM
    b0 = D_INNER // LANES
    c0 = b0 + SSM_GROUPS * D_STATE // LANES

    def chunk_body(c, carry):
        r = pl.multiple_of(c * CHUNK, CHUNK)
        dt_t = dts[pl.ds(r, CHUNK), :].T
        hi, mid, low = _split3(adts[pl.ds(r, CHUNK), :])
        a_cs = _dot(tril, hi) + _dot(tril, mid) + _dot(tril, low)
        a_cs_t = a_cs.T
        w_end_t = dt_t * jnp.exp(a_cs_t[:, CHUNK - 1:CHUNK] - a_cs_t)
        cdec = jnp.exp(a_cs[CHUNK - 1:CHUNK, :])
        for g in range(SSM_GROUPS):
            bm_t = xact[b0 + g, pl.ds(r, CHUNK), :].T
            cm_g = xact[c0 + g, pl.ds(r, CHUNK), :]
            cb = _dot(cm_g.astype(BF16), bm_t.astype(BF16))
            for jp in range(HEADS_PER_GROUP // 2):
                sl = g * (HEADS_PER_GROUP // 2) + jp
                xs_p = xact[sl, pl.ds(r, CHUNK), :]
                st_p = state[:, sl * LANES:(sl + 1) * LANES]
                y_p = None
                s_new = None
                for par in range(2):
                    hd = 2 * sl + par
                    keep = lo if par == 0 else jnp.logical_not(lo)
                    xs_m = jnp.where(keep, xs_p, 0.0).astype(BF16)
                    st_m = jnp.where(keep, st_p, 0.0).astype(BF16)
                    a_l = jnp.broadcast_to(a_cs[:, hd:hd + 1], (CHUNK, CHUNK))
                    a_s = a_cs_t[hd:hd + 1, :]
                    decay = jnp.exp(jnp.where(causal, a_l - a_s, -jnp.inf))
                    m_in = cb * decay * dt_t[hd:hd + 1, :]
                    c_off = cm_g * jnp.exp(a_l)
                    lhs = jnp.concatenate([m_in, c_off], axis=1).astype(BF16)
                    rhs = jnp.concatenate([xs_m, st_m], axis=0)
                    y_h = _dot(lhs, rhs)
                    b_dec_t = (bm_t * w_end_t[hd:hd + 1, :]).astype(BF16)
                    s_h = _dot(b_dec_t, xs_m)
                    y_p = y_h if y_p is None else y_p + y_h
                    s_new = s_h if s_new is None else s_new + s_h
                cd_p = jnp.where(lo, cdec[:, 2 * sl:2 * sl + 1], cdec[:, 2 * sl + 1:2 * sl + 2])
                state[:, sl * LANES:(sl + 1) * LANES] = cd_p * st_p + s_new
                yscr[pl.ds(r, CHUNK), sl * LANES:(sl + 1) * LANES] = (
                    y_p + xs_p * dskip_ref[:, sl * LANES:(sl + 1) * LANES])
        return carry

    lax.fori_loop(0, tm // CHUNK, chunk_body, 0, unroll=True)

    z = _dot(h, win_ref[0, :, 0:D_INNER])
    y = yscr[...] * _silu(z)
    gw = D_INNER // SSM_GROUPS
    parts = []
    for g in range(SSM_GROUPS):
        yg = y[:, g * gw:(g + 1) * gw]
        ms = jnp.mean(yg * yg, axis=-1, keepdims=True)
        parts.append((yg * lax.rsqrt(ms + NORM_EPS) * ng_ref[:, g * gw:(g + 1) * gw]).astype(BF16))
    yn = jnp.concatenate(parts, axis=1)
    o_ref[...] = x + gate * _dot(yn, wo_ref[...])


def _ssd(x, mod, norm_g, w_in, w_dt, conv_w, conv_b, dt_bias, a_log, d_skip, norm_y, w_out):
    s = x.shape[0]
    tm = TM_SSD
    assert s % tm == 0 and tm % CHUNK == 0
    return pl.pallas_call(
        functools.partial(_ssd_kernel, tm=tm),
        grid=(s // tm,),
        in_specs=[
            _row_spec(tm, D_MODEL),
            _const_spec(mod.shape),
            _const_spec(norm_g.shape),
            _layer_spec(w_in.shape, 0),
            _const_spec(w_dt.shape),
            _const_spec(conv_w.shape),
            _const_spec(conv_b.shape),
            _const_spec(dt_bias.shape),
            _const_spec(a_log.shape),
            _const_spec(d_skip.shape),
            _const_spec(norm_y.shape),
            _const_spec(w_out.shape),
        ],
        out_specs=_row_spec(tm, D_MODEL),
        out_shape=jax.ShapeDtypeStruct((s, D_MODEL), F32),
        scratch_shapes=[
            pltpu.VMEM((XBC_WIDTH // LANES, CONV_HALO + tm, LANES), F32),
            pltpu.VMEM((XBC_WIDTH // LANES, tm, LANES), F32),
            pltpu.VMEM((tm, LANES), F32),
            pltpu.VMEM((tm, LANES), F32),
            pltpu.VMEM((D_STATE, D_INNER), F32),
            pltpu.VMEM((tm, D_INNER), F32),
        ],
        compiler_params=pltpu.CompilerParams(
            dimension_semantics=("arbitrary",), vmem_limit_bytes=VMEM_LIMIT_BYTES),
        name="ssd",
    )(x, mod, norm_g, w_in, w_dt, conv_w, conv_b, dt_bias, a_log, d_skip, norm_y, w_out)


def _pad_lanes(v, width=LANES):
    return jnp.pad(v, ((0, 0), (0, width - v.shape[1])))


def kernel(x, c, positions, w_mod, b_mod, norm_mix, norm_ffn, w_in_e, b_in_e, attn_sinks, conf_conv_w,
           conf_conv_b, conf_ln_g, conf_ln_b, w_out_e, w_in_o, ssm_conv_w, ssm_conv_b, ssm_dt_bias, ssm_a_log,
           ssm_d, ssm_norm_g, w_out_o, ffn_w_gate, ffn_conv_w, ffn_conv_b, ffn_w_val, ffn_w_down, final_norm):
    b, s, d = x.shape
    assert b == 1 and d == D_MODEL and w_mod.shape[0] == 2
    xs = x.reshape(s, d)
    pos = positions.reshape(1, s)

    mod = _modulation(c, w_mod, b_mod)

    half = ROT_DIM // 2
    inv_freq = 1.0 / (ROPE_THETA ** (jnp.arange(half, dtype=F32) * 2.0 / ROT_DIM))
    freq = jnp.concatenate([inv_freq, inv_freq]).reshape(ROT_DIM, 1)
    hl = np.arange(LANES) % HEAD_DIM
    fr = np.arange(ROT_DIM)[:, None]
    is_cos = fr < half
    place = np.concatenate([
        np.where(is_cos & (hl[None, :] < ROT_DIM) & (hl[None, :] % half == fr), 1.0, 0.0),
        np.where(~is_cos & (hl[None, :] >= half) & (hl[None, :] < ROT_DIM) & (hl[None, :] % half == fr - half), 1.0, 0.0),
        np.where(~is_cos & (hl[None, :] < half) & (hl[None, :] == fr - half), -1.0, 0.0)], axis=1)
    place = jnp.asarray(place, BF16)

    row = lambda v: v.reshape(1, -1)
    final_g = row(final_norm)

    xs = _mix0(xs, pos, mod[0], row(norm_mix[0]), freq, place, w_in_e[0].astype(BF16), row(b_in_e[0]), attn_sinks[0],
               jnp.pad(conf_conv_w[0], ((0, 1), (0, 0))), row(conf_conv_b[0]), row(conf_ln_g[0]),
               row(conf_ln_b[0]), w_out_e[0].astype(BF16))
    w_gate, w_val, w_down = ffn_w_gate.astype(BF16), ffn_w_val.astype(BF16), ffn_w_down.astype(BF16)
    xs = _ffn(xs, mod[0], row(norm_ffn[0]), w_gate, ffn_conv_w[0], row(ffn_conv_b[0]), w_val, w_down, final_g, 0, False)

    w_dt = _pad_lanes(w_in_o[0][:, D_INNER + XBC_WIDTH:]).astype(BF16)
    d_skip = jnp.repeat(ssm_d[0], SSM_HEADDIM).reshape(1, D_INNER)
    xs = _ssd(xs, mod[1], row(norm_mix[1]), w_in_o.astype(BF16), w_dt, ssm_conv_w[0], row(ssm_conv_b[0]),
              _pad_lanes(row(ssm_dt_bias[0])), _pad_lanes(row(ssm_a_log[0])), d_skip, row(ssm_norm_g[0]),
              w_out_o[0].astype(BF16))
    xs = _ffn(xs, mod[1], row(norm_ffn[1]), w_gate, ffn_conv_w[1], row(ffn_conv_b[1]), w_val, w_down, final_g, 1, True)
    return xs.reshape(b, s, d)
```

```python
import functools

import jax
import jax.numpy as jnp
import numpy as np
from jax import lax
from jax.experimental import pallas as pl
from jax.experimental.pallas import tpu as pltpu

F32 = jnp.float32
BF16 = jnp.bfloat16

D_MODEL = 1024
N_Q_HEADS = 8
N_KV_HEADS = 2
HEAD_DIM = 64
ATTN_WIDTH = N_Q_HEADS * HEAD_DIM
KV_WIDTH = N_KV_HEADS * HEAD_DIM
WINDOW = 128
ROT_DIM = HEAD_DIM // 4
ROPE_THETA = 500000.0
CONF_WIDTH = D_MODEL // 2
CONF_KERNEL = 31
E_IN_COLS = ATTN_WIDTH + 2 * KV_WIDTH + 2 * CONF_WIDTH
D_INNER = 2 * D_MODEL
SSM_HEADDIM = 64
SSM_HEADS = D_INNER // SSM_HEADDIM
SSM_GROUPS = 4
HEADS_PER_GROUP = SSM_HEADS // SSM_GROUPS
D_STATE = 128
SSM_CONV = 4
CHUNK = 128
XBC_WIDTH = D_INNER + 2 * SSM_GROUPS * D_STATE
D_FF = 2816
FFN_CONV = 3
NORM_EPS = 1e-6
LN_EPS = 1e-5

LANES = 128
SUBLANES = 8
VMEM_LIMIT_BYTES = 56 * 1024 * 1024

TM_MIX0 = 512
TM_FFN = 256
TM_SSD = 256
CONF_ROWS = 32
CONF_HALO = 32
CONV_HALO = 8


def _const_spec(shape):
    nd = len(shape)
    return pl.BlockSpec(shape, lambda i: (0,) * nd, pipeline_mode=pl.Buffered(1))


def _layer_spec(shape, layer):
    return pl.BlockSpec((1,) + tuple(shape[1:]), lambda i: (layer, 0, 0), pipeline_mode=pl.Buffered(1))


def _row_spec(tm, width):
    return pl.BlockSpec((tm, width), lambda i: (i, 0))


def _gated(a, g):
    ha = 0.5 * a
    return ha + ha * jnp.tanh(0.5 * g)


def _silu(v):
    return _gated(v, v)


def _slab(s):
    return slice(s * LANES, (s + 1) * LANES)


def _rows2(start, n):
    return pl.ds(start, n, stride=2)


def _spread(main, side):
    out, done = [], 0
    for k, unit in enumerate(main):
        out.append(unit)
        want = ((k + 1) * len(side)) // len(main)
        out.extend(side[done:want])
        done = want
    return out


def _mod_norm(x, gain, scale, shift):
    ms = jnp.mean(x * x, axis=-1, keepdims=True)
    return (x * lax.rsqrt(ms + NORM_EPS)) * (gain * (1.0 + scale)) + shift


def _dot(a, b):
    return jnp.dot(a, b, preferred_element_type=F32)


def _mod_kernel(c_ref, w_ref, b_ref, o_ref):
    c = c_ref[...]
    s = _silu(c)
    o_ref[0] = jnp.sum(w_ref[0] * s, axis=0, keepdims=True) + b_ref[0]


def _modulation(c, w_mod, b_mod):
    depth, d, n = w_mod.shape
    nb = 1536
    assert n % nb == 0
    return pl.pallas_call(
        _mod_kernel,
        grid=(depth, n // nb),
        in_specs=[
            pl.BlockSpec((d, 1), lambda i, j: (0, 0)),
            pl.BlockSpec((1, d, nb), lambda i, j: (i, 0, j)),
            pl.BlockSpec((1, 1, nb), lambda i, j: (i, 0, j)),
        ],
        out_specs=pl.BlockSpec((1, 1, nb), lambda i, j: (i, 0, j)),
        out_shape=jax.ShapeDtypeStruct((depth, 1, n), F32),
        compiler_params=pltpu.CompilerParams(
            dimension_semantics=("arbitrary", "arbitrary"), vmem_limit_bytes=VMEM_LIMIT_BYTES),
        name="mod",
    )(c.reshape(d, 1), w_mod, b_mod.reshape(depth, 1, n))


def _mix0_kernel(x_ref, pos_ref, mod_ref, nm_ref, freq_ref, place_ref, win_ref, bin_ref, sink_ref, cw_ref, cb_ref,
                 lng_ref, lnb_ref, wout_ref, o_ref, kbuf, vbuf, uslab, hslab, merged, *, tm):
    i = pl.program_id(0)

    @pl.when(i == 0)
    def _():
        kbuf[:, 0:WINDOW, :] = jnp.zeros((4, WINDOW, LANES), BF16)
        vbuf[:, 0:WINDOW, :] = jnp.zeros((4, WINDOW, LANES), BF16)
        uslab[:, 0:CONF_HALO, :] = jnp.zeros((CONF_WIDTH // LANES, CONF_HALO, LANES), F32)

    x = x_ref[...]
    shift = mod_ref[:, 0:D_MODEL]
    scale = mod_ref[:, D_MODEL:2 * D_MODEL]
    gate = mod_ref[:, 2 * D_MODEL:3 * D_MODEL]
    h = _mod_norm(x, nm_ref[...], scale, shift).astype(BF16)
    proj = _dot(h, win_ref[...]) + bin_ref[...]

    ang = freq_ref[...] * pos_ref[...].astype(F32)
    frow = lax.broadcasted_iota(jnp.int32, ang.shape, 0)
    tab = jnp.where(frow < ROT_DIM // 2, jnp.cos(ang), jnp.sin(ang))
    tabs = None
    for part in _split3(tab):
        d = lax.dot_general(part, place_ref[...], (((0,), (0,)), ((), ())), preferred_element_type=F32)
        tabs = d if tabs is None else tabs + d
    lane = lax.broadcasted_iota(jnp.int32, (1, LANES), 1)
    cosv = tabs[:, 0:LANES] + jnp.where((lane & (HEAD_DIM - 1)) >= ROT_DIM, 1.0, 0.0)
    sin_up = tabs[:, LANES:2 * LANES]
    sin_dn = tabs[:, 2 * LANES:3 * LANES]
    lo = lane < HEAD_DIM

    def rope(t):
        return (t * cosv + pltpu.roll(t, ROT_DIM // 2, 1) * sin_up
                + pltpu.roll(t, LANES - ROT_DIM // 2, 1) * sin_dn)

    k = rope(proj[:, ATTN_WIDTH:ATTN_WIDTH + KV_WIDTH])
    v = proj[:, ATTN_WIDTH + KV_WIDTH:ATTN_WIDTH + 2 * KV_WIDTH]
    for buf, t in ((kbuf, k), (vbuf, v)):
        t_sw = pltpu.roll(t, HEAD_DIM, 1)
        buf[0, WINDOW:WINDOW + tm, :] = jnp.where(lo, t, 0.0).astype(BF16)
        buf[1, WINDOW:WINDOW + tm, :] = jnp.where(lo, 0.0, t_sw).astype(BF16)
        buf[2, WINDOW:WINDOW + tm, :] = jnp.where(lo, t_sw, 0.0).astype(BF16)
        buf[3, WINDOW:WINDOW + tm, :] = jnp.where(lo, 0.0, t).astype(BF16)

    q_slabs = [rope(proj[:, s * LANES:(s + 1) * LANES] * (HEAD_DIM ** -0.5)).astype(BF16)
               for s in range(ATTN_WIDTH // LANES)]

    row = lax.broadcasted_iota(jnp.int32, (2 * WINDOW, 1), 0)
    qi = row & (WINDOW - 1)
    kj = lax.broadcasted_iota(jnp.int32, (1, 2 * WINDOW), 1)
    band = (kj > qi) & (kj <= qi + WINDOW)
    band_first = band & ((kj >= WINDOW) | (i > 0))

    def attn_unit(b, kh):
        def unit():
            r0 = b * WINDOW
            mask = band_first if b == 0 else band
            qs = jnp.concatenate([q_slabs[2 * kh][r0:r0 + WINDOW], q_slabs[2 * kh + 1][r0:r0 + WINDOW]], axis=0)
            acc = None
            for par in range(2):
                sink = jnp.where(row < WINDOW, sink_ref[4 * kh + par], sink_ref[4 * kh + 2 + par])
                kk = kbuf[2 * kh + par, r0:r0 + 2 * WINDOW, :]
                s = lax.dot_general(qs, kk, (((1,), (1,)), ((), ())), preferred_element_type=F32)
                s = jnp.where(mask, s, -jnp.inf)
                m = jnp.maximum(jnp.max(s, axis=-1, keepdims=True), sink)
                p = jnp.exp(s - m)
                denom = jnp.sum(p, axis=-1, keepdims=True) + jnp.exp(sink - m)
                o = _dot(p.astype(BF16), vbuf[2 * kh + par, r0:r0 + 2 * WINDOW, :]) * (1.0 / denom)
                acc = o if acc is None else acc + o
            merged[r0:r0 + WINDOW, 2 * kh * LANES:(2 * kh + 1) * LANES] = acc[0:WINDOW].astype(BF16)
            merged[r0:r0 + WINDOW, (2 * kh + 1) * LANES:(2 * kh + 2) * LANES] = acc[WINDOW:].astype(BF16)
        return unit

    c0 = ATTN_WIDTH + 2 * KV_WIDTH
    n_cs = CONF_WIDTH // LANES
    for s in range(n_cs):
        uslab[s, CONF_HALO:CONF_HALO + tm, :] = _gated(proj[:, c0 + s * LANES:c0 + (s + 1) * LANES],
                                                      proj[:, c0 + CONF_WIDTH + s * LANES:
                                                           c0 + CONF_WIDTH + (s + 1) * LANES])
    tap0 = CONF_HALO - (CONF_KERNEL - 1)
    half = CONF_ROWS // 2

    def conf_unit(r0):
        def unit():
            for par in range(2):
                accs = []
                for s in range(n_cs):
                    acc = jnp.zeros((half, LANES), F32) + cb_ref[:, _slab(s)]
                    for kk in range(CONF_KERNEL):
                        acc = acc + cw_ref[kk:kk + 1, _slab(s)] * uslab[s, _rows2(r0 + tap0 + kk + par, half), :]
                    accs.append(acc)
                mu = sum(jnp.sum(a, axis=-1, keepdims=True) for a in accs) * (1.0 / CONF_WIDTH)
                cens = [a - mu for a in accs]
                var = sum(jnp.sum(cn * cn, axis=-1, keepdims=True) for cn in cens) * (1.0 / CONF_WIDTH)
                rstd = lax.rsqrt(var + LN_EPS)
                for s in range(n_cs):
                    hn = cens[s] * rstd * lng_ref[:, _slab(s)] + lnb_ref[:, _slab(s)]
                    hslab[s, _rows2(r0 + par, half), :] = _silu(hn)
        return unit

    conf_units = [conf_unit(r0) for r0 in range(0, tm, CONF_ROWS)]
    attn_units = [attn_unit(b, kh) for b in range(tm // WINDOW) for kh in range(N_KV_HEADS)]
    for unit in _spread(conf_units, attn_units):
        unit()

    kbuf[:, 0:WINDOW, :] = kbuf[:, tm:tm + WINDOW, :]
    vbuf[:, 0:WINDOW, :] = vbuf[:, tm:tm + WINDOW, :]
    uslab[:, 0:CONF_HALO, :] = uslab[:, tm:tm + CONF_HALO, :]
    for s in range(n_cs):
        merged[:, ATTN_WIDTH + s * LANES:ATTN_WIDTH + (s + 1) * LANES] = hslab[s].astype(BF16)

    y = _dot(merged[...], wout_ref[...])
    o_ref[...] = x + gate * y


def _mix0(x, pos, mod, norm_g, freq, place, w_in, b_in, sinks, conv_w, conv_b, ln_g, ln_b, w_out):
    s = x.shape[0]
    tm = TM_MIX0
    assert s % tm == 0 and tm % WINDOW == 0 and tm % CONF_ROWS == 0
    return pl.pallas_call(
        functools.partial(_mix0_kernel, tm=tm),
        grid=(s // tm,),
        in_specs=[
            _row_spec(tm, D_MODEL),
            pl.BlockSpec((1, tm), lambda i: (0, i)),
            _const_spec(mod.shape),
            _const_spec(norm_g.shape),
            _const_spec(freq.shape),
            _const_spec(place.shape),
            _const_spec(w_in.shape),
            _const_spec(b_in.shape),
            pl.BlockSpec(memory_space=pltpu.SMEM),
            _const_spec(conv_w.shape),
            _const_spec(conv_b.shape),
            _const_spec(ln_g.shape),
            _const_spec(ln_b.shape),
            _const_spec(w_out.shape),
        ],
        out_specs=_row_spec(tm, D_MODEL),
        out_shape=jax.ShapeDtypeStruct((s, D_MODEL), F32),
        scratch_shapes=[
            pltpu.VMEM((4, WINDOW + tm, LANES), BF16),
            pltpu.VMEM((4, WINDOW + tm, LANES), BF16),
            pltpu.VMEM((CONF_WIDTH // LANES, CONF_HALO + tm, LANES), F32),
            pltpu.VMEM((CONF_WIDTH // LANES, tm, LANES), F32),
            pltpu.VMEM((tm, D_MODEL), BF16),
        ],
        compiler_params=pltpu.CompilerParams(
            dimension_semantics=("arbitrary",), vmem_limit_bytes=VMEM_LIMIT_BYTES),
        name="mix0",
    )(x, pos, mod, norm_g, freq, place, w_in, b_in, sinks, conv_w, conv_b, ln_g, ln_b, w_out)


def _ffn_kernel(x_ref, mod_ref, nf_ref, wg_ref, wv_ref, cw_ref, cb_ref, wd_ref, fn_ref, o_ref, gext,
                *, tm, final_norm):
    i = pl.program_id(0)

    @pl.when(i == 0)
    def _():
        gext[0:CONV_HALO, :] = jnp.zeros((CONV_HALO, D_FF), F32)

    x = x_ref[...]
    shift = mod_ref[:, 3 * D_MODEL:4 * D_MODEL]
    scale = mod_ref[:, 4 * D_MODEL:5 * D_MODEL]
    gate = mod_ref[:, 5 * D_MODEL:6 * D_MODEL]
    h = _mod_norm(x, nf_ref[...], scale, shift).astype(BF16)
    gext[CONV_HALO:CONV_HALO + tm, :] = _dot(h, wg_ref[0])
    val = _dot(h, wv_ref[0])
    g = cb_ref[...]
    for kk in range(FFN_CONV):
        r = CONV_HALO - (FFN_CONV - 1) + kk
        g = g + cw_ref[kk:kk + 1, :] * gext[r:r + tm, :]
    gext[0:CONV_HALO, :] = gext[tm:tm + CONV_HALO, :]
    act = (_silu(g) * val).astype(BF16)
    y = x + gate * _dot(act, wd_ref[0])
    if final_norm:
        ms = jnp.mean(y * y, axis=-1, keepdims=True)
        y = y * lax.rsqrt(ms + NORM_EPS) * fn_ref[...]
    o_ref[...] = y


def _ffn(x, mod, norm_g, w_gate, conv_w, conv_b, w_val, w_down, final_g, layer, final_norm):
    s = x.shape[0]
    tm = TM_FFN
    assert s % tm == 0
    return pl.pallas_call(
        functools.partial(_ffn_kernel, tm=tm, final_norm=final_norm),
        grid=(s // tm,),
        in_specs=[
            _row_spec(tm, D_MODEL),
            _const_spec(mod.shape),
            _const_spec(norm_g.shape),
            _layer_spec(w_gate.shape, layer),
            _layer_spec(w_val.shape, layer),
            _const_spec(conv_w.shape),
            _const_spec(conv_b.shape),
            _layer_spec(w_down.shape, layer),
            _const_spec(final_g.shape),
        ],
        out_specs=_row_spec(tm, D_MODEL),
        out_shape=jax.ShapeDtypeStruct((s, D_MODEL), F32),
        scratch_shapes=[pltpu.VMEM((CONV_HALO + tm, D_FF), F32)],
        compiler_params=pltpu.CompilerParams(
            dimension_semantics=("arbitrary",), vmem_limit_bytes=VMEM_LIMIT_BYTES),
        name="ffn_final" if final_norm else "ffn",
    )(x, mod, norm_g, w_gate, w_val, conv_w, conv_b, w_down, final_g)


def _split3(v):
    hi = v.astype(BF16)
    r1 = v - hi.astype(F32)
    mid = r1.astype(BF16)
    lo = (r1 - mid.astype(F32)).astype(BF16)
    return hi, mid, lo


def _ssd_kernel(x_ref, mod_ref, nm_ref, win_ref, wdt_ref, cw_ref, cb_ref, dtb_ref, alog_ref, dskip_ref,
                ng_ref, wo_ref, o_ref, xslab, xact, dts, adts, state, yscr, zbuf, *, tm):
    i = pl.program_id(0)

    @pl.when(i == 0)
    def _():
        xslab[:, 0:CONV_HALO, :] = jnp.zeros((XBC_WIDTH // LANES, CONV_HALO, LANES), F32)
        state[...] = jnp.zeros((D_STATE, D_INNER), F32)

    x = x_ref[...]
    shift = mod_ref[:, 0:D_MODEL]
    scale = mod_ref[:, D_MODEL:2 * D_MODEL]
    gate = mod_ref[:, 2 * D_MODEL:3 * D_MODEL]
    h = _mod_norm(x, nm_ref[...], scale, shift).astype(BF16)

    n_s = XBC_WIDTH // LANES
    xraw = _dot(h, win_ref[0, :, D_INNER:D_INNER + XBC_WIDTH])
    for s in range(n_s):
        xslab[s, CONV_HALO:CONV_HALO + tm, :] = xraw[:, _slab(s)]
    tap0 = CONV_HALO - (SSM_CONV - 1)
    for s in range(n_s):
        for par in range(2):
            acc = cb_ref[:, _slab(s)]
            for kk in range(SSM_CONV):
                acc = acc + cw_ref[kk:kk + 1, _slab(s)] * xslab[s, _rows2(tap0 + kk + par, tm // 2), :]
            xact[s, _rows2(par, tm // 2), :] = _silu(acc)
    xslab[:, 0:CONV_HALO, :] = xslab[:, tm:tm + CONV_HALO, :]

    dt_in = _dot(h, wdt_ref[...]) + dtb_ref[...]
    dt = jnp.maximum(dt_in, 0.0) + jnp.log1p(jnp.exp(-jnp.abs(dt_in)))
    dts[...] = dt
    adts[...] = dt * (-jnp.exp(alog_ref[...]))

    ri = lax.broadcasted_iota(jnp.int32, (CHUNK, CHUNK), 0)
    ci = lax.broadcasted_iota(jnp.int32, (CHUNK, CHUNK), 1)
    causal = ri >= ci
    tril = jnp.where(causal, 1.0, 0.0).astype(BF16)
    lo = lax.broadcasted_iota(jnp.int32, (1, LANES), 1) < SSM_HEADDIM
    b0 = D_INNER // LANES
    c0 = b0 + SSM_GROUPS * D_STATE // LANES

    ctx = {}

    def chunk_unit(c):
        def unit():
            rows = slice(c * CHUNK, (c + 1) * CHUNK)
            dt_t = dts[rows, :].T
            hi, mid, low = _split3(adts[rows, :])
            a_cs = _dot(tril, hi) + _dot(tril, mid) + _dot(tril, low)
            a_cs_t = a_cs.T
            ctx["chunk"] = dict(
                dt_t=dt_t, a_cs=a_cs, a_cs_t=a_cs_t,
                w_end_t=dt_t * jnp.exp(a_cs_t[:, CHUNK - 1:CHUNK] - a_cs_t),
                cdec=jnp.exp(a_cs[CHUNK - 1:CHUNK, :]))
        return unit

    def group_unit(c, g):
        def unit():
            rows = slice(c * CHUNK, (c + 1) * CHUNK)
            bm_t = xact[b0 + g, rows, :].T
            cm_g = xact[c0 + g, rows, :]
            ctx["group"] = (bm_t, cm_g, _dot(cm_g.astype(BF16), bm_t.astype(BF16)))
        return unit

    def pair_unit(c, sl):
        def unit():
            rows = slice(c * CHUNK, (c + 1) * CHUNK)
            ch = ctx["chunk"]
            bm_t, cm_g, cb = ctx["group"]
            xs_p = xact[sl, rows, :]
            st_p = state[:, _slab(sl)]
            y_p = None
            s_new = None
            for par in range(2):
                hd = 2 * sl + par
                keep = lo if par == 0 else jnp.logical_not(lo)
                xs_m = jnp.where(keep, xs_p, 0.0).astype(BF16)
                st_m = jnp.where(keep, st_p, 0.0).astype(BF16)
                a_l = jnp.broadcast_to(ch["a_cs"][:, hd:hd + 1], (CHUNK, CHUNK))
                a_s = ch["a_cs_t"][hd:hd + 1, :]
                decay = jnp.exp(jnp.where(causal, a_l - a_s, -jnp.inf))
                m_in = cb * decay * ch["dt_t"][hd:hd + 1, :]
                c_off = cm_g * jnp.exp(a_l)
                lhs = jnp.concatenate([m_in, c_off], axis=1).astype(BF16)
                rhs = jnp.concatenate([xs_m, st_m], axis=0)
                y_h = _dot(lhs, rhs)
                b_dec_t = (bm_t * ch["w_end_t"][hd:hd + 1, :]).astype(BF16)
                s_h = _dot(b_dec_t, xs_m)
                y_p = y_h if y_p is None else y_p + y_h
                s_new = s_h if s_new is None else s_new + s_h
            cd_p = jnp.where(lo, ch["cdec"][:, 2 * sl:2 * sl + 1], ch["cdec"][:, 2 * sl + 1:2 * sl + 2])
            state[:, _slab(sl)] = cd_p * st_p + s_new
            yscr[rows, _slab(sl)] = y_p + xs_p * dskip_ref[:, _slab(sl)]
        return unit

    def z_unit(j):
        def unit():
            cols = slice(2 * j * LANES, (2 * j + 2) * LANES)
            zbuf[:, cols] = _dot(h, win_ref[0, :, cols])
        return unit

    pairs_per_group = HEADS_PER_GROUP // 2
    scan_units = []
    for c in range(tm // CHUNK):
        scan_units.append(chunk_unit(c))
        for g in range(SSM_GROUPS):
            scan_units.append(group_unit(c, g))
            scan_units.extend(pair_unit(c, g * pairs_per_group + jp) for jp in range(pairs_per_group))
    for unit in _spread(scan_units, [z_unit(j) for j in range(D_INNER // (2 * LANES))]):
        unit()

    y = yscr[...] * _silu(zbuf[...])
    gw = D_INNER // SSM_GROUPS
    parts = []
    for g in range(SSM_GROUPS):
        yg = y[:, g * gw:(g + 1) * gw]
        ms = jnp.mean(yg * yg, axis=-1, keepdims=True)
        parts.append((yg * lax.rsqrt(ms + NORM_EPS) * ng_ref[:, g * gw:(g + 1) * gw]).astype(BF16))
    yn = jnp.concatenate(parts, axis=1)
    o_ref[...] = x + gate * _dot(yn, wo_ref[...])


def _ssd(x, mod, norm_g, w_in, w_dt, conv_w, conv_b, dt_bias, a_log, d_skip, norm_y, w_out):
    s = x.shape[0]
    tm = TM_SSD
    assert s % tm == 0 and tm % CHUNK == 0
    return pl.pallas_call(
        functools.partial(_ssd_kernel, tm=tm),
        grid=(s // tm,),
        in_specs=[
            _row_spec(tm, D_MODEL),
            _const_spec(mod.shape),
            _const_spec(norm_g.shape),
            _layer_spec(w_in.shape, 0),
            _const_spec(w_dt.shape),
            _const_spec(conv_w.shape),
            _const_spec(conv_b.shape),
            _const_spec(dt_bias.shape),
            _const_spec(a_log.shape),
            _const_spec(d_skip.shape),
            _const_spec(norm_y.shape),
            _const_spec(w_out.shape),
        ],
        out_specs=_row_spec(tm, D_MODEL),
        out_shape=jax.ShapeDtypeStruct((s, D_MODEL), F32),
        scratch_shapes=[
            pltpu.VMEM((XBC_WIDTH // LANES, CONV_HALO + tm, LANES), F32),
            pltpu.VMEM((XBC_WIDTH // LANES, tm, LANES), F32),
            pltpu.VMEM((tm, LANES), F32),
            pltpu.VMEM((tm, LANES), F32),
            pltpu.VMEM((D_STATE, D_INNER), F32),
            pltpu.VMEM((tm, D_INNER), F32),
            pltpu.VMEM((tm, D_INNER), F32),
        ],
        compiler_params=pltpu.CompilerParams(
            dimension_semantics=("arbitrary",), vmem_limit_bytes=VMEM_LIMIT_BYTES),
        name="ssd",
    )(x, mod, norm_g, w_in, w_dt, conv_w, conv_b, dt_bias, a_log, d_skip, norm_y, w_out)


def _pad_lanes(v, width=LANES):
    return jnp.pad(v, ((0, 0), (0, width - v.shape[1])))


def kernel(x, c, positions, w_mod, b_mod, norm_mix, norm_ffn, w_in_e, b_in_e, attn_sinks, conf_conv_w,
           conf_conv_b, conf_ln_g, conf_ln_b, w_out_e, w_in_o, ssm_conv_w, ssm_conv_b, ssm_dt_bias, ssm_a_log,
           ssm_d, ssm_norm_g, w_out_o, ffn_w_gate, ffn_conv_w, ffn_conv_b, ffn_w_val, ffn_w_down, final_norm):
    b, s, d = x.shape
    assert b == 1 and d == D_MODEL and w_mod.shape[0] == 2
    xs = x.reshape(s, d)
    pos = positions.reshape(1, s)

    mod = _modulation(c, w_mod, b_mod)

    half = ROT_DIM // 2
    inv_freq = 1.0 / (ROPE_THETA ** (jnp.arange(half, dtype=F32) * 2.0 / ROT_DIM))
    freq = jnp.concatenate([inv_freq, inv_freq]).reshape(ROT_DIM, 1)
    hl = np.arange(LANES) % HEAD_DIM
    fr = np.arange(ROT_DIM)[:, None]
    is_cos = fr < half
    place = np.concatenate([
        np.where(is_cos & (hl[None, :] < ROT_DIM) & (hl[None, :] % half == fr), 1.0, 0.0),
        np.where(~is_cos & (hl[None, :] >= half) & (hl[None, :] < ROT_DIM) & (hl[None, :] % half == fr - half), 1.0, 0.0),
        np.where(~is_cos & (hl[None, :] < half) & (hl[None, :] == fr - half), -1.0, 0.0)], axis=1)
    place = jnp.asarray(place, BF16)

    row = lambda v: v.reshape(1, -1)
    final_g = row(final_norm)

    xs = _mix0(xs, pos, mod[0], row(norm_mix[0]), freq, place, w_in_e[0].astype(BF16), row(b_in_e[0]), attn_sinks[0],
               jnp.pad(conf_conv_w[0], ((0, 1), (0, 0))), row(conf_conv_b[0]), row(conf_ln_g[0]),
               row(conf_ln_b[0]), w_out_e[0].astype(BF16))
    w_gate, w_val, w_down = ffn_w_gate.astype(BF16), ffn_w_val.astype(BF16), ffn_w_down.astype(BF16)
    xs = _ffn(xs, mod[0], row(norm_ffn[0]), w_gate, ffn_conv_w[0], row(ffn_conv_b[0]), w_val, w_down, final_g, 0, False)

    w_dt = _pad_lanes(w_in_o[0][:, D_INNER + XBC_WIDTH:]).astype(BF16)
    d_skip = jnp.repeat(ssm_d[0], SSM_HEADDIM).reshape(1, D_INNER)
    xs = _ssd(xs, mod[1], row(norm_mix[1]), w_in_o.astype(BF16), w_dt, ssm_conv_w[0], row(ssm_conv_b[0]),
              _pad_lanes(row(ssm_dt_bias[0])), _pad_lanes(row(ssm_a_log[0])), d_skip, row(ssm_norm_g[0]),
              w_out_o[0].astype(BF16))
    xs = _ffn(xs, mod[1], row(norm_ffn[1]), w_gate, ffn_conv_w[1], row(ffn_conv_b[1]), w_val, w_down, final_g, 1, True)
    return xs.reshape(b, s, d)
```

```python
import functools

import jax
import jax.numpy as jnp
import numpy as np
from jax import lax
from jax.experimental import pallas as pl
from jax.experimental.pallas import tpu as pltpu

F32 = jnp.float32
BF16 = jnp.bfloat16

D_MODEL = 1024
N_Q_HEADS = 8
N_KV_HEADS = 2
HEAD_DIM = 64
ATTN_WIDTH = N_Q_HEADS * HEAD_DIM
KV_WIDTH = N_KV_HEADS * HEAD_DIM
WINDOW = 128
ROT_DIM = HEAD_DIM // 4
ROPE_THETA = 500000.0
CONF_WIDTH = D_MODEL // 2
CONF_KERNEL = 31
E_IN_COLS = ATTN_WIDTH + 2 * KV_WIDTH + 2 * CONF_WIDTH
D_INNER = 2 * D_MODEL
SSM_HEADDIM = 64
SSM_HEADS = D_INNER // SSM_HEADDIM
SSM_GROUPS = 4
HEADS_PER_GROUP = SSM_HEADS // SSM_GROUPS
D_STATE = 128
SSM_CONV = 4
CHUNK = 128
XBC_WIDTH = D_INNER + 2 * SSM_GROUPS * D_STATE
D_FF = 2816
FFN_CONV = 3
NORM_EPS = 1e-6
LN_EPS = 1e-5

LANES = 128
SUBLANES = 8
VMEM_LIMIT_BYTES = 56 * 1024 * 1024

TM_MIX0 = 512
TM_FFN = 512
TM_SSD = 256
CONF_ROWS = 32
CONF_HALO = 32
CONV_HALO = 8


def _const_spec(shape):
    nd = len(shape)
    return pl.BlockSpec(shape, lambda i: (0,) * nd, pipeline_mode=pl.Buffered(1))


def _layer_spec(shape, layer):
    return pl.BlockSpec((1,) + tuple(shape[1:]), lambda i: (layer, 0, 0), pipeline_mode=pl.Buffered(1))


def _row_spec(tm, width):
    return pl.BlockSpec((tm, width), lambda i: (i, 0))


def _gated(a, g):
    ha = 0.5 * a
    return ha + ha * jnp.tanh(0.5 * g)


def _silu(v):
    return _gated(v, v)


def _slab(s):
    return slice(s * LANES, (s + 1) * LANES)


def _rows2(start, n):
    return pl.ds(start, n, stride=2)


def _spread(main, side):
    out, done = [], 0
    for k, unit in enumerate(main):
        out.append(unit)
        want = ((k + 1) * len(side)) // len(main)
        out.extend(side[done:want])
        done = want
    return out


def _mod_norm(x, gain, scale, shift):
    ms = jnp.mean(x * x, axis=-1, keepdims=True)
    return (x * lax.rsqrt(ms + NORM_EPS)) * (gain * (1.0 + scale)) + shift


def _dot(a, b):
    return jnp.dot(a, b, preferred_element_type=F32)


def _mod_kernel(c_ref, w_ref, b_ref, o_ref):
    c = c_ref[...]
    s = _silu(c)
    o_ref[0] = jnp.sum(w_ref[0] * s, axis=0, keepdims=True) + b_ref[0]


def _modulation(c, w_mod, b_mod):
    depth, d, n = w_mod.shape
    nb = 1536
    assert n % nb == 0
    return pl.pallas_call(
        _mod_kernel,
        grid=(depth, n // nb),
        in_specs=[
            pl.BlockSpec((d, 1), lambda i, j: (0, 0)),
            pl.BlockSpec((1, d, nb), lambda i, j: (i, 0, j)),
            pl.BlockSpec((1, 1, nb), lambda i, j: (i, 0, j)),
        ],
        out_specs=pl.BlockSpec((1, 1, nb), lambda i, j: (i, 0, j)),
        out_shape=jax.ShapeDtypeStruct((depth, 1, n), F32),
        compiler_params=pltpu.CompilerParams(
            dimension_semantics=("arbitrary", "arbitrary"), vmem_limit_bytes=VMEM_LIMIT_BYTES),
        name="mod",
    )(c.reshape(d, 1), w_mod, b_mod.reshape(depth, 1, n))


def _mix0_kernel(x_ref, pos_ref, mod_ref, nm_ref, freq_ref, place_ref, win_ref, bin_ref, sink_ref, cw_ref, cb_ref,
                 lng_ref, lnb_ref, wout_ref, o_ref, kbuf, vbuf, uslab, hslab, merged, *, tm):
    i = pl.program_id(0)

    @pl.when(i == 0)
    def _():
        kbuf[:, 0:WINDOW, :] = jnp.zeros((4, WINDOW, LANES), BF16)
        vbuf[:, 0:WINDOW, :] = jnp.zeros((4, WINDOW, LANES), BF16)
        uslab[:, 0:CONF_HALO, :] = jnp.zeros((CONF_WIDTH // LANES, CONF_HALO, LANES), F32)

    x = x_ref[...]
    shift = mod_ref[:, 0:D_MODEL]
    scale = mod_ref[:, D_MODEL:2 * D_MODEL]
    gate = mod_ref[:, 2 * D_MODEL:3 * D_MODEL]
    h = _mod_norm(x, nm_ref[...], scale, shift).astype(BF16)
    proj = _dot(h, win_ref[...]) + bin_ref[...]

    ang = freq_ref[...] * pos_ref[...].astype(F32)
    frow = lax.broadcasted_iota(jnp.int32, ang.shape, 0)
    tab = jnp.where(frow < ROT_DIM // 2, jnp.cos(ang), jnp.sin(ang))
    tabs = None
    for part in _split3(tab):
        d = lax.dot_general(part, place_ref[...], (((0,), (0,)), ((), ())), preferred_element_type=F32)
        tabs = d if tabs is None else tabs + d
    lane = lax.broadcasted_iota(jnp.int32, (1, LANES), 1)
    cosv = tabs[:, 0:LANES] + jnp.where((lane & (HEAD_DIM - 1)) >= ROT_DIM, 1.0, 0.0)
    sin_up = tabs[:, LANES:2 * LANES]
    sin_dn = tabs[:, 2 * LANES:3 * LANES]
    lo = lane < HEAD_DIM

    def rope(t):
        return (t * cosv + pltpu.roll(t, ROT_DIM // 2, 1) * sin_up
                + pltpu.roll(t, LANES - ROT_DIM // 2, 1) * sin_dn)

    k = rope(proj[:, ATTN_WIDTH:ATTN_WIDTH + KV_WIDTH])
    v = proj[:, ATTN_WIDTH + KV_WIDTH:ATTN_WIDTH + 2 * KV_WIDTH]
    for buf, t in ((kbuf, k), (vbuf, v)):
        t_sw = pltpu.roll(t, HEAD_DIM, 1)
        buf[0, WINDOW:WINDOW + tm, :] = jnp.where(lo, t, 0.0).astype(BF16)
        buf[1, WINDOW:WINDOW + tm, :] = jnp.where(lo, 0.0, t_sw).astype(BF16)
        buf[2, WINDOW:WINDOW + tm, :] = jnp.where(lo, t_sw, 0.0).astype(BF16)
        buf[3, WINDOW:WINDOW + tm, :] = jnp.where(lo, 0.0, t).astype(BF16)

    q_slabs = [rope(proj[:, s * LANES:(s + 1) * LANES] * (HEAD_DIM ** -0.5)).astype(BF16)
               for s in range(ATTN_WIDTH // LANES)]

    row = lax.broadcasted_iota(jnp.int32, (2 * WINDOW, 1), 0)
    qi = row & (WINDOW - 1)
    kj = lax.broadcasted_iota(jnp.int32, (1, 2 * WINDOW), 1)
    band = (kj > qi) & (kj <= qi + WINDOW)
    band_first = band & ((kj >= WINDOW) | (i > 0))

    def attn_unit(b, kh):
        def unit():
            r0 = b * WINDOW
            mask = band_first if b == 0 else band
            qs = jnp.concatenate([q_slabs[2 * kh][r0:r0 + WINDOW], q_slabs[2 * kh + 1][r0:r0 + WINDOW]], axis=0)
            acc = None
            for par in range(2):
                sink = jnp.where(row < WINDOW, sink_ref[4 * kh + par], sink_ref[4 * kh + 2 + par])
                kk = kbuf[2 * kh + par, r0:r0 + 2 * WINDOW, :]
                s = lax.dot_general(qs, kk, (((1,), (1,)), ((), ())), preferred_element_type=F32)
                s = jnp.where(mask, s, -jnp.inf)
                m = jnp.maximum(jnp.max(s, axis=-1, keepdims=True), sink)
                p = jnp.exp(s - m)
                denom = jnp.sum(p, axis=-1, keepdims=True) + jnp.exp(sink - m)
                o = _dot(p.astype(BF16), vbuf[2 * kh + par, r0:r0 + 2 * WINDOW, :]) * (1.0 / denom)
                acc = o if acc is None else acc + o
            merged[r0:r0 + WINDOW, 2 * kh * LANES:(2 * kh + 1) * LANES] = acc[0:WINDOW].astype(BF16)
            merged[r0:r0 + WINDOW, (2 * kh + 1) * LANES:(2 * kh + 2) * LANES] = acc[WINDOW:].astype(BF16)
        return unit

    c0 = ATTN_WIDTH + 2 * KV_WIDTH
    n_cs = CONF_WIDTH // LANES
    for s in range(n_cs):
        uslab[s, CONF_HALO:CONF_HALO + tm, :] = _gated(proj[:, c0 + s * LANES:c0 + (s + 1) * LANES],
                                                      proj[:, c0 + CONF_WIDTH + s * LANES:
                                                           c0 + CONF_WIDTH + (s + 1) * LANES])
    tap0 = CONF_HALO - (CONF_KERNEL - 1)
    half = CONF_ROWS // 2

    def conf_unit(r0):
        def unit():
            for par in range(2):
                accs = []
                for s in range(n_cs):
                    acc = jnp.zeros((half, LANES), F32) + cb_ref[:, _slab(s)]
                    for kk in range(CONF_KERNEL):
                        acc = acc + cw_ref[kk:kk + 1, _slab(s)] * uslab[s, _rows2(r0 + tap0 + kk + par, half), :]
                    accs.append(acc)
                mu = sum(jnp.sum(a, axis=-1, keepdims=True) for a in accs) * (1.0 / CONF_WIDTH)
                cens = [a - mu for a in accs]
                var = sum(jnp.sum(cn * cn, axis=-1, keepdims=True) for cn in cens) * (1.0 / CONF_WIDTH)
                rstd = lax.rsqrt(var + LN_EPS)
                for s in range(n_cs):
                    hn = cens[s] * rstd * lng_ref[:, _slab(s)] + lnb_ref[:, _slab(s)]
                    hslab[s, _rows2(r0 + par, half), :] = _silu(hn)
        return unit

    conf_units = [conf_unit(r0) for r0 in range(0, tm, CONF_ROWS)]
    attn_units = [attn_unit(b, kh) for b in range(tm // WINDOW) for kh in range(N_KV_HEADS)]
    for unit in _spread(conf_units, attn_units):
        unit()

    kbuf[:, 0:WINDOW, :] = kbuf[:, tm:tm + WINDOW, :]
    vbuf[:, 0:WINDOW, :] = vbuf[:, tm:tm + WINDOW, :]
    uslab[:, 0:CONF_HALO, :] = uslab[:, tm:tm + CONF_HALO, :]
    for s in range(n_cs):
        merged[:, ATTN_WIDTH + s * LANES:ATTN_WIDTH + (s + 1) * LANES] = hslab[s].astype(BF16)

    y = _dot(merged[...], wout_ref[...])
    o_ref[...] = x + gate * y


def _mix0(x, pos, mod, norm_g, freq, place, w_in, b_in, sinks, conv_w, conv_b, ln_g, ln_b, w_out):
    s = x.shape[0]
    tm = TM_MIX0
    assert s % tm == 0 and tm % WINDOW == 0 and tm % CONF_ROWS == 0
    return pl.pallas_call(
        functools.partial(_mix0_kernel, tm=tm),
        grid=(s // tm,),
        in_specs=[
            _row_spec(tm, D_MODEL),
            pl.BlockSpec((1, tm), lambda i: (0, i)),
            _const_spec(mod.shape),
            _const_spec(norm_g.shape),
            _const_spec(freq.shape),
            _const_spec(place.shape),
            _const_spec(w_in.shape),
            _const_spec(b_in.shape),
            pl.BlockSpec(memory_space=pltpu.SMEM),
            _const_spec(conv_w.shape),
            _const_spec(conv_b.shape),
            _const_spec(ln_g.shape),
            _const_spec(ln_b.shape),
            _const_spec(w_out.shape),
        ],
        out_specs=_row_spec(tm, D_MODEL),
        out_shape=jax.ShapeDtypeStruct((s, D_MODEL), F32),
        scratch_shapes=[
            pltpu.VMEM((4, WINDOW + tm, LANES), BF16),
            pltpu.VMEM((4, WINDOW + tm, LANES), BF16),
            pltpu.VMEM((CONF_WIDTH // LANES, CONF_HALO + tm, LANES), F32),
            pltpu.VMEM((CONF_WIDTH // LANES, tm, LANES), F32),
            pltpu.VMEM((tm, D_MODEL), BF16),
        ],
        compiler_params=pltpu.CompilerParams(
            dimension_semantics=("arbitrary",), vmem_limit_bytes=VMEM_LIMIT_BYTES),
        name="mix0",
    )(x, pos, mod, norm_g, freq, place, w_in, b_in, sinks, conv_w, conv_b, ln_g, ln_b, w_out)


def _ffn_kernel(x_ref, mod_ref, nf_ref, wg_ref, wv_ref, cw_ref, cb_ref, wd_ref, fn_ref, o_ref, gext, actb,
                *, tm, final_norm):
    i = pl.program_id(0)

    @pl.when(i == 0)
    def _():
        gext[0:CONV_HALO, :] = jnp.zeros((CONV_HALO, D_FF), F32)

    shift = mod_ref[:, 3 * D_MODEL:4 * D_MODEL]
    scale = mod_ref[:, 4 * D_MODEL:5 * D_MODEL]
    gate = mod_ref[:, 5 * D_MODEL:6 * D_MODEL]
    hm = tm // 2
    hs = {}

    def norm_unit(hf):
        def unit():
            hs[hf] = _mod_norm(x_ref[hf * hm:(hf + 1) * hm, :], nf_ref[...], scale, shift).astype(BF16)
        return unit

    def tile_unit(hf, j):
        def unit():
            cols = slice(2 * j * LANES, (2 * j + 2) * LANES)
            r0 = CONV_HALO + hf * hm
            gext[r0:r0 + hm, cols] = _dot(hs[hf], wg_ref[0, :, cols])
            val = _dot(hs[hf], wv_ref[0, :, cols])
            g = cb_ref[:, cols]
            for kk in range(FFN_CONV):
                r = r0 - (FFN_CONV - 1) + kk
                g = g + cw_ref[kk:kk + 1, cols] * gext[r:r + hm, cols]
            actb[hf * hm:(hf + 1) * hm, cols] = (_silu(g) * val).astype(BF16)
        return unit

    def down_unit(hf):
        def unit():
            rows = slice(hf * hm, (hf + 1) * hm)
            y = x_ref[rows, :] + gate * _dot(actb[rows, :], wd_ref[0])
            if final_norm:
                ms = jnp.mean(y * y, axis=-1, keepdims=True)
                y = y * lax.rsqrt(ms + NORM_EPS) * fn_ref[...]
            o_ref[rows, :] = y
        return unit

    n_t = D_FF // (2 * LANES)
    tiles_a = [tile_unit(0, j) for j in range(n_t)]
    tiles_b = [tile_unit(1, j) for j in range(n_t)]
    units = [norm_unit(0)] + tiles_a[:2] + [norm_unit(1)] + tiles_a[2:] + [down_unit(0)] + tiles_b + [down_unit(1)]
    for unit in units:
        unit()
    gext[0:CONV_HALO, :] = gext[tm:tm + CONV_HALO, :]


def _ffn(x, mod, norm_g, w_gate, conv_w, conv_b, w_val, w_down, final_g, layer, final_norm):
    s = x.shape[0]
    tm = TM_FFN
    assert s % tm == 0
    return pl.pallas_call(
        functools.partial(_ffn_kernel, tm=tm, final_norm=final_norm),
        grid=(s // tm,),
        in_specs=[
            _row_spec(tm, D_MODEL),
            _const_spec(mod.shape),
            _const_spec(norm_g.shape),
            _layer_spec(w_gate.shape, layer),
            _layer_spec(w_val.shape, layer),
            _const_spec(conv_w.shape),
            _const_spec(conv_b.shape),
            _layer_spec(w_down.shape, layer),
            _const_spec(final_g.shape),
        ],
        out_specs=_row_spec(tm, D_MODEL),
        out_shape=jax.ShapeDtypeStruct((s, D_MODEL), F32),
        scratch_shapes=[pltpu.VMEM((CONV_HALO + tm, D_FF), F32), pltpu.VMEM((tm, D_FF), BF16)],
        compiler_params=pltpu.CompilerParams(
            dimension_semantics=("arbitrary",), vmem_limit_bytes=VMEM_LIMIT_BYTES),
        name="ffn_final" if final_norm else "ffn",
    )(x, mod, norm_g, w_gate, w_val, conv_w, conv_b, w_down, final_g)


def _split3(v):
    hi = v.astype(BF16)
    r1 = v - hi.astype(F32)
    mid = r1.astype(BF16)
    lo = (r1 - mid.astype(F32)).astype(BF16)
    return hi, mid, lo


def _ssd_kernel(x_ref, mod_ref, nm_ref, win_ref, wdt_ref, cw_ref, cb_ref, dtb_ref, alog_ref, dskip_ref,
                ng_ref, wo_ref, o_ref, xslab, xact, dts, adts, state, yscr, zbuf, *, tm):
    i = pl.program_id(0)

    @pl.when(i == 0)
    def _():
        xslab[:, 0:CONV_HALO, :] = jnp.zeros((XBC_WIDTH // LANES, CONV_HALO, LANES), F32)
        state[...] = jnp.zeros((D_STATE, D_INNER), F32)

    x = x_ref[...]
    shift = mod_ref[:, 0:D_MODEL]
    scale = mod_ref[:, D_MODEL:2 * D_MODEL]
    gate = mod_ref[:, 2 * D_MODEL:3 * D_MODEL]
    h = _mod_norm(x, nm_ref[...], scale, shift).astype(BF16)

    n_s = XBC_WIDTH // LANES
    xraw = _dot(h, win_ref[0, :, D_INNER:D_INNER + XBC_WIDTH])
    for s in range(n_s):
        xslab[s, CONV_HALO:CONV_HALO + tm, :] = xraw[:, _slab(s)]
    tap0 = CONV_HALO - (SSM_CONV - 1)
    for s in range(n_s):
        for par in range(2):
            acc = cb_ref[:, _slab(s)]
            for kk in range(SSM_CONV):
                acc = acc + cw_ref[kk:kk + 1, _slab(s)] * xslab[s, _rows2(tap0 + kk + par, tm // 2), :]
            xact[s, _rows2(par, tm // 2), :] = _silu(acc)
    xslab[:, 0:CONV_HALO, :] = xslab[:, tm:tm + CONV_HALO, :]

    dt_in = _dot(h, wdt_ref[...]) + dtb_ref[...]
    dt = jnp.maximum(dt_in, 0.0) + jnp.log1p(jnp.exp(-jnp.abs(dt_in)))
    dts[...] = dt
    adts[...] = dt * (-jnp.exp(alog_ref[...]))

    ri = lax.broadcasted_iota(jnp.int32, (CHUNK, CHUNK), 0)
    ci = lax.broadcasted_iota(jnp.int32, (CHUNK, CHUNK), 1)
    causal = ri >= ci
    tril = jnp.where(causal, 1.0, 0.0).astype(BF16)
    lo = lax.broadcasted_iota(jnp.int32, (1, LANES), 1) < SSM_HEADDIM
    b0 = D_INNER // LANES
    c0 = b0 + SSM_GROUPS * D_STATE // LANES

    ctx = {}

    def chunk_unit(c):
        def unit():
            rows = slice(c * CHUNK, (c + 1) * CHUNK)
            dt_t = dts[rows, :].T
            hi, mid, low = _split3(adts[rows, :])
            a_cs = _dot(tril, hi) + _dot(tril, mid) + _dot(tril, low)
            a_cs_t = a_cs.T
            ctx["chunk"] = dict(
                dt_t=dt_t, a_cs=a_cs, a_cs_t=a_cs_t,
                w_end_t=dt_t * jnp.exp(a_cs_t[:, CHUNK - 1:CHUNK] - a_cs_t),
                cdec=jnp.exp(a_cs[CHUNK - 1:CHUNK, :]))
        return unit

    def group_unit(c, g):
        def unit():
            rows = slice(c * CHUNK, (c + 1) * CHUNK)
            bm_t = xact[b0 + g, rows, :].T
            cm_g = xact[c0 + g, rows, :]
            bm_t = bm_t.astype(BF16)
            cm_g = cm_g.astype(BF16)
            ctx["group"] = (bm_t, cm_g, _dot(cm_g, bm_t).astype(BF16))
        return unit

    def pair_unit(c, sl):
        def unit():
            rows = slice(c * CHUNK, (c + 1) * CHUNK)
            ch = ctx["chunk"]
            bm_t, cm_g, cb = ctx["group"]
            xs_p = xact[sl, rows, :]
            st_p = state[:, _slab(sl)]
            y_p = None
            s_new = None
            for par in range(2):
                hd = 2 * sl + par
                keep = lo if par == 0 else jnp.logical_not(lo)
                xs_m = jnp.where(keep, xs_p, 0.0).astype(BF16)
                st_m = jnp.where(keep, st_p, 0.0).astype(BF16)
                a_l = jnp.broadcast_to(ch["a_cs"][:, hd:hd + 1], (CHUNK, CHUNK))
                a_s = ch["a_cs_t"][hd:hd + 1, :]
                decay = jnp.exp(jnp.where(causal, a_l - a_s, -jnp.inf))
                m_in = cb * decay.astype(BF16) * ch["dt_t"][hd:hd + 1, :].astype(BF16)
                c_off = cm_g * jnp.exp(a_l).astype(BF16)
                lhs = jnp.concatenate([m_in, c_off], axis=1)
                rhs = jnp.concatenate([xs_m, st_m], axis=0)
                y_h = _dot(lhs, rhs)
                b_dec_t = bm_t * ch["w_end_t"][hd:hd + 1, :].astype(BF16)
                s_h = _dot(b_dec_t, xs_m)
                y_p = y_h if y_p is None else y_p + y_h
                s_new = s_h if s_new is None else s_new + s_h
            cd_p = jnp.where(lo, ch["cdec"][:, 2 * sl:2 * sl + 1], ch["cdec"][:, 2 * sl + 1:2 * sl + 2])
            state[:, _slab(sl)] = cd_p * st_p + s_new
            yscr[rows, _slab(sl)] = y_p + xs_p * dskip_ref[:, _slab(sl)]
        return unit

    def z_unit(j):
        def unit():
            cols = slice(2 * j * LANES, (2 * j + 2) * LANES)
            zbuf[:, cols] = _dot(h, win_ref[0, :, cols])
        return unit

    pairs_per_group = HEADS_PER_GROUP // 2
    scan_units = []
    for c in range(tm // CHUNK):
        scan_units.append(chunk_unit(c))
        for g in range(SSM_GROUPS):
            scan_units.append(group_unit(c, g))
            scan_units.extend(pair_unit(c, g * pairs_per_group + jp) for jp in range(pairs_per_group))
    for unit in _spread(scan_units, [z_unit(j) for j in range(D_INNER // (2 * LANES))]):
        unit()

    y = yscr[...] * _silu(zbuf[...])
    gw = D_INNER // SSM_GROUPS
    parts = []
    for g in range(SSM_GROUPS):
        yg = y[:, g * gw:(g + 1) * gw]
        ms = jnp.mean(yg * yg, axis=-1, keepdims=True)
        parts.append((yg * lax.rsqrt(ms + NORM_EPS) * ng_ref[:, g * gw:(g + 1) * gw]).astype(BF16))
    yn = jnp.concatenate(parts, axis=1)
    o_ref[...] = x + gate * _dot(yn, wo_ref[...])


def _ssd(x, mod, norm_g, w_in, w_dt, conv_w, conv_b, dt_bias, a_log, d_skip, norm_y, w_out):
    s = x.shape[0]
    tm = TM_SSD
    assert s % tm == 0 and tm % CHUNK == 0
    return pl.pallas_call(
        functools.partial(_ssd_kernel, tm=tm),
        grid=(s // tm,),
        in_specs=[
            _row_spec(tm, D_MODEL),
            _const_spec(mod.shape),
            _const_spec(norm_g.shape),
            _layer_spec(w_in.shape, 0),
            _const_spec(w_dt.shape),
            _const_spec(conv_w.shape),
            _const_spec(conv_b.shape),
            _const_spec(dt_bias.shape),
            _const_spec(a_log.shape),
            _const_spec(d_skip.shape),
            _const_spec(norm_y.shape),
            _const_spec(w_out.shape),
        ],
        out_specs=_row_spec(tm, D_MODEL),
        out_shape=jax.ShapeDtypeStruct((s, D_MODEL), F32),
        scratch_shapes=[
            pltpu.VMEM((XBC_WIDTH // LANES, CONV_HALO + tm, LANES), F32),
            pltpu.VMEM((XBC_WIDTH // LANES, tm, LANES), F32),
            pltpu.VMEM((tm, LANES), F32),
            pltpu.VMEM((tm, LANES), F32),
            pltpu.VMEM((D_STATE, D_INNER), F32),
            pltpu.VMEM((tm, D_INNER), F32),
            pltpu.VMEM((tm, D_INNER), F32),
        ],
        compiler_params=pltpu.CompilerParams(
            dimension_semantics=("arbitrary",), vmem_limit_bytes=VMEM_LIMIT_BYTES),
        name="ssd",
    )(x, mod, norm_g, w_in, w_dt, conv_w, conv_b, dt_bias, a_log, d_skip, norm_y, w_out)


def _pad_lanes(v, width=LANES):
    return jnp.pad(v, ((0, 0), (0, width - v.shape[1])))


def kernel(x, c, positions, w_mod, b_mod, norm_mix, norm_ffn, w_in_e, b_in_e, attn_sinks, conf_conv_w,
           conf_conv_b, conf_ln_g, conf_ln_b, w_out_e, w_in_o, ssm_conv_w, ssm_conv_b, ssm_dt_bias, ssm_a_log,
           ssm_d, ssm_norm_g, w_out_o, ffn_w_gate, ffn_conv_w, ffn_conv_b, ffn_w_val, ffn_w_down, final_norm):
    b, s, d = x.shape
    assert b == 1 and d == D_MODEL and w_mod.shape[0] == 2
    xs = x.reshape(s, d)
    pos = positions.reshape(1, s)

    mod = _modulation(c, w_mod, b_mod)

    half = ROT_DIM // 2
    inv_freq = 1.0 / (ROPE_THETA ** (jnp.arange(half, dtype=F32) * 2.0 / ROT_DIM))
    freq = jnp.concatenate([inv_freq, inv_freq]).reshape(ROT_DIM, 1)
    hl = np.arange(LANES) % HEAD_DIM
    fr = np.arange(ROT_DIM)[:, None]
    is_cos = fr < half
    place = np.concatenate([
        np.where(is_cos & (hl[None, :] < ROT_DIM) & (hl[None, :] % half == fr), 1.0, 0.0),
        np.where(~is_cos & (hl[None, :] >= half) & (hl[None, :] < ROT_DIM) & (hl[None, :] % half == fr - half), 1.0, 0.0),
        np.where(~is_cos & (hl[None, :] < half) & (hl[None, :] == fr - half), -1.0, 0.0)], axis=1)
    place = jnp.asarray(place, BF16)

    row = lambda v: v.reshape(1, -1)
    final_g = row(final_norm)

    xs = _mix0(xs, pos, mod[0], row(norm_mix[0]), freq, place, w_in_e[0].astype(BF16), row(b_in_e[0]), attn_sinks[0],
               jnp.pad(conf_conv_w[0], ((0, 1), (0, 0))), row(conf_conv_b[0]), row(conf_ln_g[0]),
               row(conf_ln_b[0]), w_out_e[0].astype(BF16))
    w_gate, w_val, w_down = ffn_w_gate.astype(BF16), ffn_w_val.astype(BF16), ffn_w_down.astype(BF16)
    xs = _ffn(xs, mod[0], row(norm_ffn[0]), w_gate, ffn_conv_w[0], row(ffn_conv_b[0]), w_val, w_down, final_g, 0, False)

    w_dt = _pad_lanes(w_in_o[0][:, D_INNER + XBC_WIDTH:]).astype(BF16)
    d_skip = jnp.repeat(ssm_d[0], SSM_HEADDIM).reshape(1, D_INNER)
    xs = _ssd(xs, mod[1], row(norm_mix[1]), w_in_o.astype(BF16), w_dt, ssm_conv_w[0], row(ssm_conv_b[0]),
              _pad_lanes(row(ssm_dt_bias[0])), _pad_lanes(row(ssm_a_log[0])), d_skip, row(ssm_norm_g[0]),
              w_out_o[0].astype(BF16))
    xs = _ffn(xs, mod[1], row(norm_ffn[1]), w_gate, ffn_conv_w[1], row(ffn_conv_b[1]), w_val, w_down, final_g, 1, True)
    return xs.reshape(b, s, d)
```

```python
import functools

import jax
import jax.numpy as jnp
import numpy as np
from jax import lax
from jax.experimental import pallas as pl
from jax.experimental.pallas import tpu as pltpu

F32 = jnp.float32
BF16 = jnp.bfloat16

D_MODEL = 1024
N_Q_HEADS = 8
N_KV_HEADS = 2
HEAD_DIM = 64
ATTN_WIDTH = N_Q_HEADS * HEAD_DIM
KV_WIDTH = N_KV_HEADS * HEAD_DIM
WINDOW = 128
ROT_DIM = HEAD_DIM // 4
ROPE_THETA = 500000.0
CONF_WIDTH = D_MODEL // 2
CONF_KERNEL = 31
E_IN_COLS = ATTN_WIDTH + 2 * KV_WIDTH + 2 * CONF_WIDTH
D_INNER = 2 * D_MODEL
SSM_HEADDIM = 64
SSM_HEADS = D_INNER // SSM_HEADDIM
SSM_GROUPS = 4
HEADS_PER_GROUP = SSM_HEADS // SSM_GROUPS
D_STATE = 128
SSM_CONV = 4
CHUNK = 128
XBC_WIDTH = D_INNER + 2 * SSM_GROUPS * D_STATE
D_FF = 2816
FFN_CONV = 3
NORM_EPS = 1e-6
LN_EPS = 1e-5

LANES = 128
SUBLANES = 8
VMEM_LIMIT_BYTES = 56 * 1024 * 1024

TM_MIX0 = 1024
TM_FFN = 512
FFN_WCHUNKS = 8
TM_SSD = 512
CONF_ROWS = 32
CONF_HALO = 32
CONV_HALO = 8


def _const_spec(shape):
    nd = len(shape)
    return pl.BlockSpec(shape, lambda i: (0,) * nd, pipeline_mode=pl.Buffered(1))


def _layer_spec(shape, layer):
    return pl.BlockSpec((1,) + tuple(shape[1:]), lambda i: (layer, 0, 0), pipeline_mode=pl.Buffered(1))


def _row_spec(tm, width):
    return pl.BlockSpec((tm, width), lambda i: (i, 0))


def _gated(a, g):
    ha = 0.5 * a
    return ha + ha * jnp.tanh(0.5 * g)


def _silu(v):
    return _gated(v, v)


def _slab(s):
    return slice(s * LANES, (s + 1) * LANES)


def _rows2(start, n):
    return pl.ds(start, n, stride=2)


def _spread(main, side):
    out, done = [], 0
    for k, unit in enumerate(main):
        out.append(unit)
        want = ((k + 1) * len(side)) // len(main)
        out.extend(side[done:want])
        done = want
    return out


def _mod_norm(x, gain, scale, shift):
    ms = jnp.mean(x * x, axis=-1, keepdims=True)
    return (x * lax.rsqrt(ms + NORM_EPS)) * (gain * (1.0 + scale)) + shift


def _dot(a, b):
    return jnp.dot(a, b, preferred_element_type=F32)


def _mod_kernel(c_ref, w_ref, b_ref, o_ref):
    c = c_ref[...]
    s = _silu(c)
    o_ref[0] = jnp.sum(w_ref[0] * s, axis=0, keepdims=True) + b_ref[0]


def _modulation(c, w_mod, b_mod):
    depth, d, n = w_mod.shape
    nb = 1536
    assert n % nb == 0
    return pl.pallas_call(
        _mod_kernel,
        grid=(depth, n // nb),
        in_specs=[
            pl.BlockSpec((d, 1), lambda i, j: (0, 0)),
            pl.BlockSpec((1, d, nb), lambda i, j: (i, 0, j)),
            pl.BlockSpec((1, 1, nb), lambda i, j: (i, 0, j)),
        ],
        out_specs=pl.BlockSpec((1, 1, nb), lambda i, j: (i, 0, j)),
        out_shape=jax.ShapeDtypeStruct((depth, 1, n), F32),
        compiler_params=pltpu.CompilerParams(
            dimension_semantics=("arbitrary", "arbitrary"), vmem_limit_bytes=VMEM_LIMIT_BYTES),
        name="mod",
    )(c.reshape(d, 1), w_mod, b_mod.reshape(depth, 1, n))


def _mix0_kernel(x_ref, pos_ref, mod_ref, nm_ref, freq_ref, place_ref, win_ref, bin_ref, sink_ref, cw_ref, cb_ref,
                 lng_ref, lnb_ref, wout_ref, o_ref, kbuf, vbuf, uslab, hslab, merged, *, tm):
    i = pl.program_id(0)

    @pl.when(i == 0)
    def _():
        kbuf[:, 0:WINDOW, :] = jnp.zeros((4, WINDOW, LANES), BF16)
        vbuf[:, 0:WINDOW, :] = jnp.zeros((4, WINDOW, LANES), BF16)
        uslab[:, 0:CONF_HALO, :] = jnp.zeros((CONF_WIDTH // LANES, CONF_HALO, LANES), F32)

    x = x_ref[...]
    shift = mod_ref[:, 0:D_MODEL]
    scale = mod_ref[:, D_MODEL:2 * D_MODEL]
    gate = mod_ref[:, 2 * D_MODEL:3 * D_MODEL]
    h = _mod_norm(x, nm_ref[...], scale, shift).astype(BF16)
    proj = _dot(h, win_ref[...]) + bin_ref[...]

    ang = freq_ref[...] * pos_ref[...].astype(F32)
    frow = lax.broadcasted_iota(jnp.int32, ang.shape, 0)
    tab = jnp.where(frow < ROT_DIM // 2, jnp.cos(ang), jnp.sin(ang))
    tabs = None
    for part in _split3(tab):
        d = lax.dot_general(part, place_ref[...], (((0,), (0,)), ((), ())), preferred_element_type=F32)
        tabs = d if tabs is None else tabs + d
    lane = lax.broadcasted_iota(jnp.int32, (1, LANES), 1)
    cosv = tabs[:, 0:LANES] + jnp.where((lane & (HEAD_DIM - 1)) >= ROT_DIM, 1.0, 0.0)
    sin_up = tabs[:, LANES:2 * LANES]
    sin_dn = tabs[:, 2 * LANES:3 * LANES]
    lo = lane < HEAD_DIM

    def rope(t):
        return (t * cosv + pltpu.roll(t, ROT_DIM // 2, 1) * sin_up
                + pltpu.roll(t, LANES - ROT_DIM // 2, 1) * sin_dn)

    k = rope(proj[:, ATTN_WIDTH:ATTN_WIDTH + KV_WIDTH])
    v = proj[:, ATTN_WIDTH + KV_WIDTH:ATTN_WIDTH + 2 * KV_WIDTH]
    for buf, t in ((kbuf, k), (vbuf, v)):
        t_sw = pltpu.roll(t, HEAD_DIM, 1)
        buf[0, WINDOW:WINDOW + tm, :] = jnp.where(lo, t, 0.0).astype(BF16)
        buf[1, WINDOW:WINDOW + tm, :] = jnp.where(lo, 0.0, t_sw).astype(BF16)
        buf[2, WINDOW:WINDOW + tm, :] = jnp.where(lo, t_sw, 0.0).astype(BF16)
        buf[3, WINDOW:WINDOW + tm, :] = jnp.where(lo, 0.0, t).astype(BF16)

    q_slabs = [rope(proj[:, s * LANES:(s + 1) * LANES] * (HEAD_DIM ** -0.5)).astype(BF16)
               for s in range(ATTN_WIDTH // LANES)]

    row = lax.broadcasted_iota(jnp.int32, (2 * WINDOW, 1), 0)
    qi = row & (WINDOW - 1)
    kj = lax.broadcasted_iota(jnp.int32, (1, 2 * WINDOW), 1)
    band = (kj > qi) & (kj <= qi + WINDOW)
    band_first = band & ((kj >= WINDOW) | (i > 0))

    def attn_unit(b, kh):
        def unit():
            r0 = b * WINDOW
            mask = band_first if b == 0 else band
            qs = jnp.concatenate([q_slabs[2 * kh][r0:r0 + WINDOW], q_slabs[2 * kh + 1][r0:r0 + WINDOW]], axis=0)
            acc = None
            for par in range(2):
                sink = jnp.where(row < WINDOW, sink_ref[4 * kh + par], sink_ref[4 * kh + 2 + par])
                kk = kbuf[2 * kh + par, r0:r0 + 2 * WINDOW, :]
                s = lax.dot_general(qs, kk, (((1,), (1,)), ((), ())), preferred_element_type=F32)
                s = jnp.where(mask, s, -jnp.inf)
                m = jnp.maximum(jnp.max(s, axis=-1, keepdims=True), sink)
                p = jnp.exp(s - m)
                denom = jnp.sum(p, axis=-1, keepdims=True) + jnp.exp(sink - m)
                o = _dot(p.astype(BF16), vbuf[2 * kh + par, r0:r0 + 2 * WINDOW, :]) * (1.0 / denom)
                acc = o if acc is None else acc + o
            merged[r0:r0 + WINDOW, 2 * kh * LANES:(2 * kh + 1) * LANES] = acc[0:WINDOW].astype(BF16)
            merged[r0:r0 + WINDOW, (2 * kh + 1) * LANES:(2 * kh + 2) * LANES] = acc[WINDOW:].astype(BF16)
        return unit

    c0 = ATTN_WIDTH + 2 * KV_WIDTH
    n_cs = CONF_WIDTH // LANES
    for s in range(n_cs):
        uslab[s, CONF_HALO:CONF_HALO + tm, :] = _gated(proj[:, c0 + s * LANES:c0 + (s + 1) * LANES],
                                                      proj[:, c0 + CONF_WIDTH + s * LANES:
                                                           c0 + CONF_WIDTH + (s + 1) * LANES])
    tap0 = CONF_HALO - (CONF_KERNEL - 1)
    half = CONF_ROWS // 2

    def conf_unit(r0):
        def unit():
            for par in range(2):
                accs = []
                for s in range(n_cs):
                    acc = jnp.zeros((half, LANES), F32) + cb_ref[:, _slab(s)]
                    for kk in range(CONF_KERNEL):
                        acc = acc + cw_ref[kk:kk + 1, _slab(s)] * uslab[s, _rows2(r0 + tap0 + kk + par, half), :]
                    accs.append(acc)
                mu = sum(jnp.sum(a, axis=-1, keepdims=True) for a in accs) * (1.0 / CONF_WIDTH)
                cens = [a - mu for a in accs]
                var = sum(jnp.sum(cn * cn, axis=-1, keepdims=True) for cn in cens) * (1.0 / CONF_WIDTH)
                rstd = lax.rsqrt(var + LN_EPS)
                for s in range(n_cs):
                    hn = cens[s] * rstd * lng_ref[:, _slab(s)] + lnb_ref[:, _slab(s)]
                    hslab[s, _rows2(r0 + par, half), :] = _silu(hn)
        return unit

    conf_units = [conf_unit(r0) for r0 in range(0, tm, CONF_ROWS)]
    attn_units = [attn_unit(b, kh) for b in range(tm // WINDOW) for kh in range(N_KV_HEADS)]
    for unit in _spread(conf_units, attn_units):
        unit()

    kbuf[:, 0:WINDOW, :] = kbuf[:, tm:tm + WINDOW, :]
    vbuf[:, 0:WINDOW, :] = vbuf[:, tm:tm + WINDOW, :]
    uslab[:, 0:CONF_HALO, :] = uslab[:, tm:tm + CONF_HALO, :]
    for s in range(n_cs):
        merged[:, ATTN_WIDTH + s * LANES:ATTN_WIDTH + (s + 1) * LANES] = hslab[s].astype(BF16)

    y = _dot(merged[...], wout_ref[...])
    o_ref[...] = x + gate * y


def _mix0(x, pos, mod, norm_g, freq, place, w_in, b_in, sinks, conv_w, conv_b, ln_g, ln_b, w_out):
    s = x.shape[0]
    tm = TM_MIX0
    assert s % tm == 0 and tm % WINDOW == 0 and tm % CONF_ROWS == 0
    return pl.pallas_call(
        functools.partial(_mix0_kernel, tm=tm),
        grid=(s // tm,),
        in_specs=[
            _row_spec(tm, D_MODEL),
            pl.BlockSpec((1, tm), lambda i: (0, i)),
            _const_spec(mod.shape),
            _const_spec(norm_g.shape),
            _const_spec(freq.shape),
            _const_spec(place.shape),
            _const_spec(w_in.shape),
            _const_spec(b_in.shape),
            pl.BlockSpec(memory_space=pltpu.SMEM),
            _const_spec(conv_w.shape),
            _const_spec(conv_b.shape),
            _const_spec(ln_g.shape),
            _const_spec(ln_b.shape),
            _const_spec(w_out.shape),
        ],
        out_specs=_row_spec(tm, D_MODEL),
        out_shape=jax.ShapeDtypeStruct((s, D_MODEL), F32),
        scratch_shapes=[
            pltpu.VMEM((4, WINDOW + tm, LANES), BF16),
            pltpu.VMEM((4, WINDOW + tm, LANES), BF16),
            pltpu.VMEM((CONF_WIDTH // LANES, CONF_HALO + tm, LANES), F32),
            pltpu.VMEM((CONF_WIDTH // LANES, tm, LANES), F32),
            pltpu.VMEM((tm, D_MODEL), BF16),
        ],
        compiler_params=pltpu.CompilerParams(
            dimension_semantics=("arbitrary",), vmem_limit_bytes=VMEM_LIMIT_BYTES),
        name="mix0",
    )(x, pos, mod, norm_g, freq, place, w_in, b_in, sinks, conv_w, conv_b, ln_g, ln_b, w_out)


def _ffn_kernel(x_ref, mod_ref, nf_ref, wg32_ref, wv32_ref, cw_ref, cb_ref, wd32_ref, fn_ref, o_ref, gext, actb,
                wg_s, wv_s, wd_s, *, tm, final_norm):
    step = pl.program_id(0)

    @pl.when(step == 0)
    def _():
        gext[0:CONV_HALO, :] = jnp.zeros((CONV_HALO, D_FF), F32)

    @pl.when(step < FFN_WCHUNKS)
    def _():
        ru = pl.multiple_of(step * (D_MODEL // FFN_WCHUNKS), D_MODEL // FFN_WCHUNKS)
        rd = pl.multiple_of(step * (D_FF // FFN_WCHUNKS), D_FF // FFN_WCHUNKS)
        wg_s[pl.ds(ru, D_MODEL // FFN_WCHUNKS), :] = wg32_ref[0].astype(BF16)
        wv_s[pl.ds(ru, D_MODEL // FFN_WCHUNKS), :] = wv32_ref[0].astype(BF16)
        wd_s[pl.ds(rd, D_FF // FFN_WCHUNKS), :] = wd32_ref[0].astype(BF16)

    @pl.when(step >= FFN_WCHUNKS)
    def _():
        _ffn_tile(x_ref, mod_ref, nf_ref, wg_s, wv_s, cw_ref, cb_ref, wd_s, fn_ref, o_ref, gext, actb,
                  tm=tm, final_norm=final_norm)


def _ffn_tile(x_ref, mod_ref, nf_ref, wg_ref, wv_ref, cw_ref, cb_ref, wd_ref, fn_ref, o_ref, gext, actb,
              *, tm, final_norm):
    shift = mod_ref[:, 3 * D_MODEL:4 * D_MODEL]
    scale = mod_ref[:, 4 * D_MODEL:5 * D_MODEL]
    gate = mod_ref[:, 5 * D_MODEL:6 * D_MODEL]
    hm = tm // 2
    hs = {}

    def norm_unit(hf):
        def unit():
            hs[hf] = _mod_norm(x_ref[hf * hm:(hf + 1) * hm, :], nf_ref[...], scale, shift).astype(BF16)
        return unit

    def tile_unit(hf, j):
        def unit():
            cols = slice(2 * j * LANES, (2 * j + 2) * LANES)
            r0 = CONV_HALO + hf * hm
            gext[r0:r0 + hm, cols] = _dot(hs[hf], wg_ref[:, cols])
            val = _dot(hs[hf], wv_ref[:, cols])
            g = cb_ref[:, cols]
            for kk in range(FFN_CONV):
                r = r0 - (FFN_CONV - 1) + kk
                g = g + cw_ref[kk:kk + 1, cols] * gext[r:r + hm, cols]
            actb[hf * hm:(hf + 1) * hm, cols] = (_silu(g) * val).astype(BF16)
        return unit

    def down_unit(hf):
        def unit():
            rows = slice(hf * hm, (hf + 1) * hm)
            y = x_ref[rows, :] + gate * _dot(actb[rows, :], wd_ref[...])
            if final_norm:
                ms = jnp.mean(y * y, axis=-1, keepdims=True)
                y = y * lax.rsqrt(ms + NORM_EPS) * fn_ref[...]
            o_ref[rows, :] = y
        return unit

    n_t = D_FF // (2 * LANES)
    tiles_a = [tile_unit(0, j) for j in range(n_t)]
    tiles_b = [tile_unit(1, j) for j in range(n_t)]
    units = [norm_unit(0)] + tiles_a[:2] + [norm_unit(1)] + tiles_a[2:] + [down_unit(0)] + tiles_b + [down_unit(1)]
    for unit in units:
        unit()
    gext[0:CONV_HALO, :] = gext[tm:tm + CONV_HALO, :]


def _ffn(x, mod, norm_g, w_gate, conv_w, conv_b, w_val, w_down, final_g, layer, final_norm):
    s = x.shape[0]
    tm = TM_FFN
    nw = FFN_WCHUNKS
    assert s % tm == 0 and D_MODEL % (nw * 16) == 0 and D_FF % (nw * 16) == 0

    def tile(i):
        return (jnp.maximum(i - nw, 0), 0)

    def wchunk(i):
        return (layer, jnp.minimum(i, nw - 1), 0)

    return pl.pallas_call(
        functools.partial(_ffn_kernel, tm=tm, final_norm=final_norm),
        grid=(nw + s // tm,),
        in_specs=[
            pl.BlockSpec((tm, D_MODEL), tile),
            _const_spec(mod.shape),
            _const_spec(norm_g.shape),
            pl.BlockSpec((1, D_MODEL // nw, D_FF), wchunk),
            pl.BlockSpec((1, D_MODEL // nw, D_FF), wchunk),
            _const_spec(conv_w.shape),
            _const_spec(conv_b.shape),
            pl.BlockSpec((1, D_FF // nw, D_MODEL), wchunk),
            _const_spec(final_g.shape),
        ],
        out_specs=pl.BlockSpec((tm, D_MODEL), tile),
        out_shape=jax.ShapeDtypeStruct((s, D_MODEL), F32),
        scratch_shapes=[pltpu.VMEM((CONV_HALO + tm, D_FF), F32), pltpu.VMEM((tm, D_FF), BF16),
                        pltpu.VMEM((D_MODEL, D_FF), BF16), pltpu.VMEM((D_MODEL, D_FF), BF16),
                        pltpu.VMEM((D_FF, D_MODEL), BF16)],
        compiler_params=pltpu.CompilerParams(
            dimension_semantics=("arbitrary",), vmem_limit_bytes=VMEM_LIMIT_BYTES),
        name="ffn_final" if final_norm else "ffn",
    )(x, mod, norm_g, w_gate, w_val, conv_w, conv_b, w_down, final_g)


def _split3(v):
    hi = v.astype(BF16)
    r1 = v - hi.astype(F32)
    mid = r1.astype(BF16)
    lo = (r1 - mid.astype(F32)).astype(BF16)
    return hi, mid, lo


def _ssd_kernel(x_ref, mod_ref, nm_ref, win_ref, wdt_ref, cw_ref, cb_ref, dtb_ref, alog_ref, dskip_ref,
                ng_ref, wo_ref, o_ref, xslab, xact, dts, adts, state, yscr, zbuf, *, tm):
    i = pl.program_id(0)

    @pl.when(i == 0)
    def _():
        xslab[:, 0:CONV_HALO, :] = jnp.zeros((XBC_WIDTH // LANES, CONV_HALO, LANES), F32)
        state[...] = jnp.zeros((D_STATE, D_INNER), F32)

    x = x_ref[...]
    shift = mod_ref[:, 0:D_MODEL]
    scale = mod_ref[:, D_MODEL:2 * D_MODEL]
    gate = mod_ref[:, 2 * D_MODEL:3 * D_MODEL]
    h = _mod_norm(x, nm_ref[...], scale, shift).astype(BF16)

    n_s = XBC_WIDTH // LANES
    xraw = _dot(h, win_ref[0, :, D_INNER:D_INNER + XBC_WIDTH])
    for s in range(n_s):
        xslab[s, CONV_HALO:CONV_HALO + tm, :] = xraw[:, _slab(s)]
    tap0 = CONV_HALO - (SSM_CONV - 1)
    for s in range(n_s):
        for par in range(2):
            acc = cb_ref[:, _slab(s)]
            for kk in range(SSM_CONV):
                acc = acc + cw_ref[kk:kk + 1, _slab(s)] * xslab[s, _rows2(tap0 + kk + par, tm // 2), :]
            xact[s, _rows2(par, tm // 2), :] = _silu(acc)
    xslab[:, 0:CONV_HALO, :] = xslab[:, tm:tm + CONV_HALO, :]

    dt_in = _dot(h, wdt_ref[...]) + dtb_ref[...]
    dt = jnp.maximum(dt_in, 0.0) + jnp.log1p(jnp.exp(-jnp.abs(dt_in)))
    dts[...] = dt
    adts[...] = dt * (-jnp.exp(alog_ref[...]))

    ri = lax.broadcasted_iota(jnp.int32, (CHUNK, CHUNK), 0)
    ci = lax.broadcasted_iota(jnp.int32, (CHUNK, CHUNK), 1)
    causal = ri >= ci
    tril = jnp.where(causal, 1.0, 0.0).astype(BF16)
    lo = lax.broadcasted_iota(jnp.int32, (1, LANES), 1) < SSM_HEADDIM
    b0 = D_INNER // LANES
    c0 = b0 + SSM_GROUPS * D_STATE // LANES

    ctx = {}

    def chunk_unit(c):
        def unit():
            rows = slice(c * CHUNK, (c + 1) * CHUNK)
            dt_t = dts[rows, :].T
            hi, mid, low = _split3(adts[rows, :])
            a_cs = _dot(tril, hi) + _dot(tril, mid) + _dot(tril, low)
            a_cs_t = a_cs.T
            ctx["chunk"] = dict(
                dt_t=dt_t, a_cs=a_cs, a_cs_t=a_cs_t,
                w_end_t=dt_t * jnp.exp(a_cs_t[:, CHUNK - 1:CHUNK] - a_cs_t),
                cdec=jnp.exp(a_cs[CHUNK - 1:CHUNK, :]))
        return unit

    def group_unit(c, g):
        def unit():
            rows = slice(c * CHUNK, (c + 1) * CHUNK)
            bm_t = xact[b0 + g, rows, :].T
            cm_g = xact[c0 + g, rows, :]
            bm_t = bm_t.astype(BF16)
            cm_g = cm_g.astype(BF16)
            ctx["group"] = (bm_t, cm_g, _dot(cm_g, bm_t).astype(BF16))
        return unit

    def pair_unit(c, sl):
        def unit():
            rows = slice(c * CHUNK, (c + 1) * CHUNK)
            ch = ctx["chunk"]
            bm_t, cm_g, cb = ctx["group"]
            xs_p = xact[sl, rows, :]
            st_p = state[:, _slab(sl)]
            y_p = None
            s_new = None
            for par in range(2):
                hd = 2 * sl + par
                keep = lo if par == 0 else jnp.logical_not(lo)
                xs_m = jnp.where(keep, xs_p, 0.0).astype(BF16)
                st_m = jnp.where(keep, st_p, 0.0).astype(BF16)
                a_l = jnp.broadcast_to(ch["a_cs"][:, hd:hd + 1], (CHUNK, CHUNK))
                a_s = ch["a_cs_t"][hd:hd + 1, :]
                decay = jnp.exp(jnp.where(causal, a_l - a_s, -jnp.inf))
                m_in = cb * decay.astype(BF16) * ch["dt_t"][hd:hd + 1, :].astype(BF16)
                c_off = cm_g * jnp.exp(a_l).astype(BF16)
                lhs = jnp.concatenate([m_in, c_off], axis=1)
                rhs = jnp.concatenate([xs_m, st_m], axis=0)
                y_h = _dot(lhs, rhs)
                b_dec_t = bm_t * ch["w_end_t"][hd:hd + 1, :].astype(BF16)
                s_h = _dot(b_dec_t, xs_m)
                y_p = y_h if y_p is None else y_p + y_h
                s_new = s_h if s_new is None else s_new + s_h
            cd_p = jnp.where(lo, ch["cdec"][:, 2 * sl:2 * sl + 1], ch["cdec"][:, 2 * sl + 1:2 * sl + 2])
            state[:, _slab(sl)] = cd_p * st_p + s_new
            yscr[rows, _slab(sl)] = y_p + xs_p * dskip_ref[:, _slab(sl)]
        return unit

    def z_unit(j):
        def unit():
            cols = slice(2 * j * LANES, (2 * j + 2) * LANES)
            zbuf[:, cols] = _dot(h, win_ref[0, :, cols])
        return unit

    pairs_per_group = HEADS_PER_GROUP // 2
    scan_units = []
    for c in range(tm // CHUNK):
        scan_units.append(chunk_unit(c))
        for g in range(SSM_GROUPS):
            scan_units.append(group_unit(c, g))
            scan_units.extend(pair_unit(c, g * pairs_per_group + jp) for jp in range(pairs_per_group))
    for unit in _spread(scan_units, [z_unit(j) for j in range(D_INNER // (2 * LANES))]):
        unit()

    y = yscr[...] * _silu(zbuf[...])
    gw = D_INNER // SSM_GROUPS
    parts = []
    for g in range(SSM_GROUPS):
        yg = y[:, g * gw:(g + 1) * gw]
        ms = jnp.mean(yg * yg, axis=-1, keepdims=True)
        parts.append((yg * lax.rsqrt(ms + NORM_EPS) * ng_ref[:, g * gw:(g + 1) * gw]).astype(BF16))
    yn = jnp.concatenate(parts, axis=1)
    o_ref[...] = x + gate * _dot(yn, wo_ref[...])


def _ssd(x, mod, norm_g, w_in, w_dt, conv_w, conv_b, dt_bias, a_log, d_skip, norm_y, w_out):
    s = x.shape[0]
    tm = TM_SSD
    assert s % tm == 0 and tm % CHUNK == 0
    return pl.pallas_call(
        functools.partial(_ssd_kernel, tm=tm),
        grid=(s // tm,),
        in_specs=[
            _row_spec(tm, D_MODEL),
            _const_spec(mod.shape),
            _const_spec(norm_g.shape),
            _layer_spec(w_in.shape, 0),
            _const_spec(w_dt.shape),
            _const_spec(conv_w.shape),
            _const_spec(conv_b.shape),
            _const_spec(dt_bias.shape),
            _const_spec(a_log.shape),
            _const_spec(d_skip.shape),
            _const_spec(norm_y.shape),
            _const_spec(w_out.shape),
        ],
        out_specs=_row_spec(tm, D_MODEL),
        out_shape=jax.ShapeDtypeStruct((s, D_MODEL), F32),
        scratch_shapes=[
            pltpu.VMEM((XBC_WIDTH // LANES, CONV_HALO + tm, LANES), F32),
            pltpu.VMEM((XBC_WIDTH // LANES, tm, LANES), F32),
            pltpu.VMEM((tm, LANES), F32),
            pltpu.VMEM((tm, LANES), F32),
            pltpu.VMEM((D_STATE, D_INNER), F32),
            pltpu.VMEM((tm, D_INNER), F32),
            pltpu.VMEM((tm, D_INNER), F32),
        ],
        compiler_params=pltpu.CompilerParams(
            dimension_semantics=("arbitrary",), vmem_limit_bytes=VMEM_LIMIT_BYTES),
        name="ssd",
    )(x, mod, norm_g, w_in, w_dt, conv_w, conv_b, dt_bias, a_log, d_skip, norm_y, w_out)


def _pad_lanes(v, width=LANES):
    return jnp.pad(v, ((0, 0), (0, width - v.shape[1])))


def kernel(x, c, positions, w_mod, b_mod, norm_mix, norm_ffn, w_in_e, b_in_e, attn_sinks, conf_conv_w,
           conf_conv_b, conf_ln_g, conf_ln_b, w_out_e, w_in_o, ssm_conv_w, ssm_conv_b, ssm_dt_bias, ssm_a_log,
           ssm_d, ssm_norm_g, w_out_o, ffn_w_gate, ffn_conv_w, ffn_conv_b, ffn_w_val, ffn_w_down, final_norm):
    b, s, d = x.shape
    assert b == 1 and d == D_MODEL and w_mod.shape[0] == 2
    xs = x.reshape(s, d)
    pos = positions.reshape(1, s)

    mod = _modulation(c, w_mod, b_mod)

    half = ROT_DIM // 2
    inv_freq = 1.0 / (ROPE_THETA ** (jnp.arange(half, dtype=F32) * 2.0 / ROT_DIM))
    freq = jnp.concatenate([inv_freq, inv_freq]).reshape(ROT_DIM, 1)
    hl = np.arange(LANES) % HEAD_DIM
    fr = np.arange(ROT_DIM)[:, None]
    is_cos = fr < half
    place = np.concatenate([
        np.where(is_cos & (hl[None, :] < ROT_DIM) & (hl[None, :] % half == fr), 1.0, 0.0),
        np.where(~is_cos & (hl[None, :] >= half) & (hl[None, :] < ROT_DIM) & (hl[None, :] % half == fr - half), 1.0, 0.0),
        np.where(~is_cos & (hl[None, :] < half) & (hl[None, :] == fr - half), -1.0, 0.0)], axis=1)
    place = jnp.asarray(place, BF16)

    row = lambda v: v.reshape(1, -1)
    final_g = row(final_norm)

    xs = _mix0(xs, pos, mod[0], row(norm_mix[0]), freq, place, w_in_e[0].astype(BF16), row(b_in_e[0]), attn_sinks[0],
               jnp.pad(conf_conv_w[0], ((0, 1), (0, 0))), row(conf_conv_b[0]), row(conf_ln_g[0]),
               row(conf_ln_b[0]), w_out_e[0].astype(BF16))
    w_gate, w_val, w_down = ffn_w_gate, ffn_w_val, ffn_w_down
    xs = _ffn(xs, mod[0], row(norm_ffn[0]), w_gate, ffn_conv_w[0], row(ffn_conv_b[0]), w_val, w_down, final_g, 0, False)

    w_dt = _pad_lanes(w_in_o[0][:, D_INNER + XBC_WIDTH:]).astype(BF16)
    d_skip = jnp.repeat(ssm_d[0], SSM_HEADDIM).reshape(1, D_INNER)
    xs = _ssd(xs, mod[1], row(norm_mix[1]), w_in_o.astype(BF16), w_dt, ssm_conv_w[0], row(ssm_conv_b[0]),
              _pad_lanes(row(ssm_dt_bias[0])), _pad_lanes(row(ssm_a_log[0])), d_skip, row(ssm_norm_g[0]),
              w_out_o[0].astype(BF16))
    xs = _ffn(xs, mod[1], row(norm_ffn[1]), w_gate, ffn_conv_w[1], row(ffn_conv_b[1]), w_val, w_down, final_g, 1, True)
    return xs.reshape(b, s, d)
```

```python
import functools

import jax
import jax.numpy as jnp
import numpy as np
from jax import lax
from jax.experimental import pallas as pl
from jax.experimental.pallas import tpu as pltpu

F32 = jnp.float32
BF16 = jnp.bfloat16

D_MODEL = 1024
N_Q_HEADS = 8
N_KV_HEADS = 2
HEAD_DIM = 64
ATTN_WIDTH = N_Q_HEADS * HEAD_DIM
KV_WIDTH = N_KV_HEADS * HEAD_DIM
WINDOW = 128
ROT_DIM = HEAD_DIM // 4
ROPE_THETA = 500000.0
CONF_WIDTH = D_MODEL // 2
CONF_KERNEL = 31
E_IN_COLS = ATTN_WIDTH + 2 * KV_WIDTH + 2 * CONF_WIDTH
D_INNER = 2 * D_MODEL
SSM_HEADDIM = 64
SSM_HEADS = D_INNER // SSM_HEADDIM
SSM_GROUPS = 4
HEADS_PER_GROUP = SSM_HEADS // SSM_GROUPS
D_STATE = 128
SSM_CONV = 4
CHUNK = 128
XBC_WIDTH = D_INNER + 2 * SSM_GROUPS * D_STATE
D_FF = 2816
FFN_CONV = 3
NORM_EPS = 1e-6
LN_EPS = 1e-5

LANES = 128
SUBLANES = 8
VMEM_LIMIT_BYTES = 56 * 1024 * 1024

TM_MIX0 = 1024
TM_FFN = 512
FFN_WCHUNKS = 8
TM_SSD = 512
SSD_WCHUNKS = 16
CONF_ROWS = 32
CONF_HALO = 32
CONV_HALO = 8


def _const_spec(shape):
    nd = len(shape)
    return pl.BlockSpec(shape, lambda i: (0,) * nd, pipeline_mode=pl.Buffered(1))


def _layer_spec(shape, layer):
    return pl.BlockSpec((1,) + tuple(shape[1:]), lambda i: (layer, 0, 0), pipeline_mode=pl.Buffered(1))


def _row_spec(tm, width):
    return pl.BlockSpec((tm, width), lambda i: (i, 0))


def _gated(a, g):
    ha = 0.5 * a
    return ha + ha * jnp.tanh(0.5 * g)


def _silu(v):
    return _gated(v, v)


def _slab(s):
    return slice(s * LANES, (s + 1) * LANES)


def _rows2(start, n):
    return pl.ds(start, n, stride=2)


def _spread(main, side):
    out, done = [], 0
    for k, unit in enumerate(main):
        out.append(unit)
        want = ((k + 1) * len(side)) // len(main)
        out.extend(side[done:want])
        done = want
    return out


def _mod_norm(x, gain, scale, shift):
    ms = jnp.mean(x * x, axis=-1, keepdims=True)
    return (x * lax.rsqrt(ms + NORM_EPS)) * (gain * (1.0 + scale)) + shift


def _dot(a, b):
    return jnp.dot(a, b, preferred_element_type=F32)


def _mod_kernel(c_ref, w_ref, b_ref, o_ref):
    c = c_ref[...]
    s = _silu(c)
    o_ref[0] = jnp.sum(w_ref[0] * s, axis=0, keepdims=True) + b_ref[0]


def _modulation(c, w_mod, b_mod):
    depth, d, n = w_mod.shape
    nb = 1536
    assert n % nb == 0
    return pl.pallas_call(
        _mod_kernel,
        grid=(depth, n // nb),
        in_specs=[
            pl.BlockSpec((d, 1), lambda i, j: (0, 0)),
            pl.BlockSpec((1, d, nb), lambda i, j: (i, 0, j)),
            pl.BlockSpec((1, 1, nb), lambda i, j: (i, 0, j)),
        ],
        out_specs=pl.BlockSpec((1, 1, nb), lambda i, j: (i, 0, j)),
        out_shape=jax.ShapeDtypeStruct((depth, 1, n), F32),
        compiler_params=pltpu.CompilerParams(
            dimension_semantics=("arbitrary", "arbitrary"), vmem_limit_bytes=VMEM_LIMIT_BYTES),
        name="mod",
    )(c.reshape(d, 1), w_mod, b_mod.reshape(depth, 1, n))


def _mix0_kernel(x_ref, pos_ref, mod_ref, nm_ref, freq_ref, place_ref, win_ref, bin_ref, sink_ref, cw_ref, cb_ref,
                 lng_ref, lnb_ref, wout_ref, o_ref, kbuf, vbuf, uslab, hslab, merged, *, tm):
    i = pl.program_id(0)

    @pl.when(i == 0)
    def _():
        kbuf[:, 0:WINDOW, :] = jnp.zeros((4, WINDOW, LANES), BF16)
        vbuf[:, 0:WINDOW, :] = jnp.zeros((4, WINDOW, LANES), BF16)
        uslab[:, 0:CONF_HALO, :] = jnp.zeros((CONF_WIDTH // LANES, CONF_HALO, LANES), F32)

    x = x_ref[...]
    shift = mod_ref[:, 0:D_MODEL]
    scale = mod_ref[:, D_MODEL:2 * D_MODEL]
    gate = mod_ref[:, 2 * D_MODEL:3 * D_MODEL]
    h = _mod_norm(x, nm_ref[...], scale, shift).astype(BF16)
    proj = _dot(h, win_ref[...]) + bin_ref[...]

    ang = freq_ref[...] * pos_ref[...].astype(F32)
    frow = lax.broadcasted_iota(jnp.int32, ang.shape, 0)
    tab = jnp.where(frow < ROT_DIM // 2, jnp.cos(ang), jnp.sin(ang))
    tabs = None
    for part in _split3(tab):
        d = lax.dot_general(part, place_ref[...], (((0,), (0,)), ((), ())), preferred_element_type=F32)
        tabs = d if tabs is None else tabs + d
    lane = lax.broadcasted_iota(jnp.int32, (1, LANES), 1)
    cosv = tabs[:, 0:LANES] + jnp.where((lane & (HEAD_DIM - 1)) >= ROT_DIM, 1.0, 0.0)
    sin_up = tabs[:, LANES:2 * LANES]
    sin_dn = tabs[:, 2 * LANES:3 * LANES]
    lo = lane < HEAD_DIM

    def rope(t):
        return (t * cosv + pltpu.roll(t, ROT_DIM // 2, 1) * sin_up
                + pltpu.roll(t, LANES - ROT_DIM // 2, 1) * sin_dn)

    k = rope(proj[:, ATTN_WIDTH:ATTN_WIDTH + KV_WIDTH])
    v = proj[:, ATTN_WIDTH + KV_WIDTH:ATTN_WIDTH + 2 * KV_WIDTH]
    for buf, t in ((kbuf, k), (vbuf, v)):
        t_sw = pltpu.roll(t, HEAD_DIM, 1)
        buf[0, WINDOW:WINDOW + tm, :] = jnp.where(lo, t, 0.0).astype(BF16)
        buf[1, WINDOW:WINDOW + tm, :] = jnp.where(lo, 0.0, t_sw).astype(BF16)
        buf[2, WINDOW:WINDOW + tm, :] = jnp.where(lo, t_sw, 0.0).astype(BF16)
        buf[3, WINDOW:WINDOW + tm, :] = jnp.where(lo, 0.0, t).astype(BF16)

    q_slabs = [rope(proj[:, s * LANES:(s + 1) * LANES] * (HEAD_DIM ** -0.5)).astype(BF16)
               for s in range(ATTN_WIDTH // LANES)]

    row = lax.broadcasted_iota(jnp.int32, (2 * WINDOW, 1), 0)
    qi = row & (WINDOW - 1)
    kj = lax.broadcasted_iota(jnp.int32, (1, 2 * WINDOW), 1)
    band = (kj > qi) & (kj <= qi + WINDOW)
    band_first = band & ((kj >= WINDOW) | (i > 0))

    def attn_unit(b, kh):
        def unit():
            r0 = b * WINDOW
            mask = band_first if b == 0 else band
            qs = jnp.concatenate([q_slabs[2 * kh][r0:r0 + WINDOW], q_slabs[2 * kh + 1][r0:r0 + WINDOW]], axis=0)
            acc = None
            for par in range(2):
                sink = jnp.where(row < WINDOW, sink_ref[4 * kh + par], sink_ref[4 * kh + 2 + par])
                kk = kbuf[2 * kh + par, r0:r0 + 2 * WINDOW, :]
                s = lax.dot_general(qs, kk, (((1,), (1,)), ((), ())), preferred_element_type=F32)
                s = jnp.where(mask, s, -jnp.inf)
                m = jnp.maximum(jnp.max(s, axis=-1, keepdims=True), sink)
                p = jnp.exp(s - m)
                denom = jnp.sum(p, axis=-1, keepdims=True) + jnp.exp(sink - m)
                o = _dot(p.astype(BF16), vbuf[2 * kh + par, r0:r0 + 2 * WINDOW, :]) * (1.0 / denom)
                acc = o if acc is None else acc + o
            merged[r0:r0 + WINDOW, 2 * kh * LANES:(2 * kh + 1) * LANES] = acc[0:WINDOW].astype(BF16)
            merged[r0:r0 + WINDOW, (2 * kh + 1) * LANES:(2 * kh + 2) * LANES] = acc[WINDOW:].astype(BF16)
        return unit

    c0 = ATTN_WIDTH + 2 * KV_WIDTH
    n_cs = CONF_WIDTH // LANES
    for s in range(n_cs):
        uslab[s, CONF_HALO:CONF_HALO + tm, :] = _gated(proj[:, c0 + s * LANES:c0 + (s + 1) * LANES],
                                                      proj[:, c0 + CONF_WIDTH + s * LANES:
                                                           c0 + CONF_WIDTH + (s + 1) * LANES])
    tap0 = CONF_HALO - (CONF_KERNEL - 1)
    half = CONF_ROWS // 2

    def conf_unit(r0):
        def unit():
            for par in range(2):
                accs = []
                for s in range(n_cs):
                    acc = jnp.zeros((half, LANES), F32) + cb_ref[:, _slab(s)]
                    for kk in range(CONF_KERNEL):
                        acc = acc + cw_ref[kk:kk + 1, _slab(s)] * uslab[s, _rows2(r0 + tap0 + kk + par, half), :]
                    accs.append(acc)
                mu = sum(jnp.sum(a, axis=-1, keepdims=True) for a in accs) * (1.0 / CONF_WIDTH)
                cens = [a - mu for a in accs]
                var = sum(jnp.sum(cn * cn, axis=-1, keepdims=True) for cn in cens) * (1.0 / CONF_WIDTH)
                rstd = lax.rsqrt(var + LN_EPS)
                for s in range(n_cs):
                    hn = cens[s] * rstd * lng_ref[:, _slab(s)] + lnb_ref[:, _slab(s)]
                    hslab[s, _rows2(r0 + par, half), :] = _silu(hn)
        return unit

    conf_units = [conf_unit(r0) for r0 in range(0, tm, CONF_ROWS)]
    attn_units = [attn_unit(b, kh) for b in range(tm // WINDOW) for kh in range(N_KV_HEADS)]
    for unit in _spread(conf_units, attn_units):
        unit()

    kbuf[:, 0:WINDOW, :] = kbuf[:, tm:tm + WINDOW, :]
    vbuf[:, 0:WINDOW, :] = vbuf[:, tm:tm + WINDOW, :]
    uslab[:, 0:CONF_HALO, :] = uslab[:, tm:tm + CONF_HALO, :]
    for s in range(n_cs):
        merged[:, ATTN_WIDTH + s * LANES:ATTN_WIDTH + (s + 1) * LANES] = hslab[s].astype(BF16)

    y = _dot(merged[...], wout_ref[...])
    o_ref[...] = x + gate * y


def _mix0(x, pos, mod, norm_g, freq, place, w_in, b_in, sinks, conv_w, conv_b, ln_g, ln_b, w_out):
    s = x.shape[0]
    tm = TM_MIX0
    assert s % tm == 0 and tm % WINDOW == 0 and tm % CONF_ROWS == 0
    return pl.pallas_call(
        functools.partial(_mix0_kernel, tm=tm),
        grid=(s // tm,),
        in_specs=[
            _row_spec(tm, D_MODEL),
            pl.BlockSpec((1, tm), lambda i: (0, i)),
            _const_spec(mod.shape),
            _const_spec(norm_g.shape),
            _const_spec(freq.shape),
            _const_spec(place.shape),
            _const_spec(w_in.shape),
            _const_spec(b_in.shape),
            pl.BlockSpec(memory_space=pltpu.SMEM),
            _const_spec(conv_w.shape),
            _const_spec(conv_b.shape),
            _const_spec(ln_g.shape),
            _const_spec(ln_b.shape),
            _const_spec(w_out.shape),
        ],
        out_specs=_row_spec(tm, D_MODEL),
        out_shape=jax.ShapeDtypeStruct((s, D_MODEL), F32),
        scratch_shapes=[
            pltpu.VMEM((4, WINDOW + tm, LANES), BF16),
            pltpu.VMEM((4, WINDOW + tm, LANES), BF16),
            pltpu.VMEM((CONF_WIDTH // LANES, CONF_HALO + tm, LANES), F32),
            pltpu.VMEM((CONF_WIDTH // LANES, tm, LANES), F32),
            pltpu.VMEM((tm, D_MODEL), BF16),
        ],
        compiler_params=pltpu.CompilerParams(
            dimension_semantics=("arbitrary",), vmem_limit_bytes=VMEM_LIMIT_BYTES),
        name="mix0",
    )(x, pos, mod, norm_g, freq, place, w_in, b_in, sinks, conv_w, conv_b, ln_g, ln_b, w_out)


def _ffn_kernel(x_ref, mod_ref, nf_ref, wg32_ref, wv32_ref, cw_ref, cb_ref, wd32_ref, fn_ref, o_ref, gext, actb,
                wg_s, wv_s, wd_s, *, tm, final_norm):
    step = pl.program_id(0)

    @pl.when(step == 0)
    def _():
        gext[0:CONV_HALO, :] = jnp.zeros((CONV_HALO, D_FF), F32)

    @pl.when(step < FFN_WCHUNKS)
    def _():
        ru = pl.multiple_of(step * (D_MODEL // FFN_WCHUNKS), D_MODEL // FFN_WCHUNKS)
        rd = pl.multiple_of(step * (D_FF // FFN_WCHUNKS), D_FF // FFN_WCHUNKS)
        wg_s[pl.ds(ru, D_MODEL // FFN_WCHUNKS), :] = wg32_ref[0].astype(BF16)
        wv_s[pl.ds(ru, D_MODEL // FFN_WCHUNKS), :] = wv32_ref[0].astype(BF16)
        wd_s[pl.ds(rd, D_FF // FFN_WCHUNKS), :] = wd32_ref[0].astype(BF16)

    @pl.when(step >= FFN_WCHUNKS)
    def _():
        _ffn_tile(x_ref, mod_ref, nf_ref, wg_s, wv_s, cw_ref, cb_ref, wd_s, fn_ref, o_ref, gext, actb,
                  tm=tm, final_norm=final_norm)


def _ffn_tile(x_ref, mod_ref, nf_ref, wg_ref, wv_ref, cw_ref, cb_ref, wd_ref, fn_ref, o_ref, gext, actb,
              *, tm, final_norm):
    shift = mod_ref[:, 3 * D_MODEL:4 * D_MODEL]
    scale = mod_ref[:, 4 * D_MODEL:5 * D_MODEL]
    gate = mod_ref[:, 5 * D_MODEL:6 * D_MODEL]
    hm = tm // 2
    hs = {}

    def norm_unit(hf):
        def unit():
            hs[hf] = _mod_norm(x_ref[hf * hm:(hf + 1) * hm, :], nf_ref[...], scale, shift).astype(BF16)
        return unit

    def tile_unit(hf, j):
        def unit():
            cols = slice(2 * j * LANES, (2 * j + 2) * LANES)
            r0 = CONV_HALO + hf * hm
            gext[r0:r0 + hm, cols] = _dot(hs[hf], wg_ref[:, cols])
            val = _dot(hs[hf], wv_ref[:, cols])
            g = cb_ref[:, cols]
            for kk in range(FFN_CONV):
                r = r0 - (FFN_CONV - 1) + kk
                g = g + cw_ref[kk:kk + 1, cols] * gext[r:r + hm, cols]
            actb[hf * hm:(hf + 1) * hm, cols] = (_silu(g) * val).astype(BF16)
        return unit

    def down_unit(hf):
        def unit():
            rows = slice(hf * hm, (hf + 1) * hm)
            y = x_ref[rows, :] + gate * _dot(actb[rows, :], wd_ref[...])
            if final_norm:
                ms = jnp.mean(y * y, axis=-1, keepdims=True)
                y = y * lax.rsqrt(ms + NORM_EPS) * fn_ref[...]
            o_ref[rows, :] = y
        return unit

    n_t = D_FF // (2 * LANES)
    tiles_a = [tile_unit(0, j) for j in range(n_t)]
    tiles_b = [tile_unit(1, j) for j in range(n_t)]
    units = [norm_unit(0)] + tiles_a[:2] + [norm_unit(1)] + tiles_a[2:] + [down_unit(0)] + tiles_b + [down_unit(1)]
    for unit in units:
        unit()
    gext[0:CONV_HALO, :] = gext[tm:tm + CONV_HALO, :]


def _ffn(x, mod, norm_g, w_gate, conv_w, conv_b, w_val, w_down, final_g, layer, final_norm):
    s = x.shape[0]
    tm = TM_FFN
    nw = FFN_WCHUNKS
    assert s % tm == 0 and D_MODEL % (nw * 16) == 0 and D_FF % (nw * 16) == 0

    def tile(i):
        return (jnp.maximum(i - nw, 0), 0)

    def wchunk(i):
        return (layer, jnp.minimum(i, nw - 1), 0)

    return pl.pallas_call(
        functools.partial(_ffn_kernel, tm=tm, final_norm=final_norm),
        grid=(nw + s // tm,),
        in_specs=[
            pl.BlockSpec((tm, D_MODEL), tile),
            _const_spec(mod.shape),
            _const_spec(norm_g.shape),
            pl.BlockSpec((1, D_MODEL // nw, D_FF), wchunk),
            pl.BlockSpec((1, D_MODEL // nw, D_FF), wchunk),
            _const_spec(conv_w.shape),
            _const_spec(conv_b.shape),
            pl.BlockSpec((1, D_FF // nw, D_MODEL), wchunk),
            _const_spec(final_g.shape),
        ],
        out_specs=pl.BlockSpec((tm, D_MODEL), tile),
        out_shape=jax.ShapeDtypeStruct((s, D_MODEL), F32),
        scratch_shapes=[pltpu.VMEM((CONV_HALO + tm, D_FF), F32), pltpu.VMEM((tm, D_FF), BF16),
                        pltpu.VMEM((D_MODEL, D_FF), BF16), pltpu.VMEM((D_MODEL, D_FF), BF16),
                        pltpu.VMEM((D_FF, D_MODEL), BF16)],
        compiler_params=pltpu.CompilerParams(
            dimension_semantics=("arbitrary",), vmem_limit_bytes=VMEM_LIMIT_BYTES),
        name="ffn_final" if final_norm else "ffn",
    )(x, mod, norm_g, w_gate, w_val, conv_w, conv_b, w_down, final_g)


def _split3(v):
    hi = v.astype(BF16)
    r1 = v - hi.astype(F32)
    mid = r1.astype(BF16)
    lo = (r1 - mid.astype(F32)).astype(BF16)
    return hi, mid, lo


def _ssd_kernel(x_ref, mod_ref, nm_ref, win32_ref, wdt_ref, cw_ref, cb_ref, dtb_ref, alog_ref, dskip_ref,
                ng_ref, wo32_ref, o_ref, xslab, xact, dts, adts, state, yscr, zbuf, win_s, wo_s, *, tm):
    step = pl.program_id(0)

    @pl.when(step == 0)
    def _():
        xslab[:, 0:CONV_HALO, :] = jnp.zeros((XBC_WIDTH // LANES, CONV_HALO, LANES), F32)
        state[...] = jnp.zeros((D_STATE, D_INNER), F32)

    @pl.when(step < SSD_WCHUNKS)
    def _():
        ri = pl.multiple_of(step * (D_MODEL // SSD_WCHUNKS), D_MODEL // SSD_WCHUNKS)
        ro = pl.multiple_of(step * (D_INNER // SSD_WCHUNKS), D_INNER // SSD_WCHUNKS)
        win_s[pl.ds(ri, D_MODEL // SSD_WCHUNKS), :] = win32_ref[0].astype(BF16)
        wo_s[pl.ds(ro, D_INNER // SSD_WCHUNKS), :] = wo32_ref[0].astype(BF16)

    @pl.when(step >= SSD_WCHUNKS)
    def _():
        _ssd_tile(x_ref, mod_ref, nm_ref, win_s, wdt_ref, cw_ref, cb_ref, dtb_ref, alog_ref, dskip_ref,
                  ng_ref, wo_s, o_ref, xslab, xact, dts, adts, state, yscr, zbuf, tm=tm)


def _ssd_tile(x_ref, mod_ref, nm_ref, win_ref, wdt_ref, cw_ref, cb_ref, dtb_ref, alog_ref, dskip_ref,
              ng_ref, wo_ref, o_ref, xslab, xact, dts, adts, state, yscr, zbuf, *, tm):
    x = x_ref[...]
    shift = mod_ref[:, 0:D_MODEL]
    scale = mod_ref[:, D_MODEL:2 * D_MODEL]
    gate = mod_ref[:, 2 * D_MODEL:3 * D_MODEL]
    h = _mod_norm(x, nm_ref[...], scale, shift).astype(BF16)

    n_s = XBC_WIDTH // LANES
    xraw = _dot(h, win_ref[:, D_INNER:D_INNER + XBC_WIDTH])
    for s in range(n_s):
        xslab[s, CONV_HALO:CONV_HALO + tm, :] = xraw[:, _slab(s)]
    tap0 = CONV_HALO - (SSM_CONV - 1)
    for s in range(n_s):
        for par in range(2):
            acc = cb_ref[:, _slab(s)]
            for kk in range(SSM_CONV):
                acc = acc + cw_ref[kk:kk + 1, _slab(s)] * xslab[s, _rows2(tap0 + kk + par, tm // 2), :]
            xact[s, _rows2(par, tm // 2), :] = _silu(acc)
    xslab[:, 0:CONV_HALO, :] = xslab[:, tm:tm + CONV_HALO, :]

    dt_in = _dot(h, wdt_ref[...]) + dtb_ref[...]
    dt = jnp.maximum(dt_in, 0.0) + jnp.log1p(jnp.exp(-jnp.abs(dt_in)))
    dts[...] = dt
    adts[...] = dt * (-jnp.exp(alog_ref[...]))

    ri = lax.broadcasted_iota(jnp.int32, (CHUNK, CHUNK), 0)
    ci = lax.broadcasted_iota(jnp.int32, (CHUNK, CHUNK), 1)
    causal = ri >= ci
    tril = jnp.where(causal, 1.0, 0.0).astype(BF16)
    lo = lax.broadcasted_iota(jnp.int32, (1, LANES), 1) < SSM_HEADDIM
    b0 = D_INNER // LANES
    c0 = b0 + SSM_GROUPS * D_STATE // LANES

    ctx = {}

    def chunk_unit(c):
        def unit():
            rows = slice(c * CHUNK, (c + 1) * CHUNK)
            dt_t = dts[rows, :].T
            hi, mid, low = _split3(adts[rows, :])
            a_cs = _dot(tril, hi) + _dot(tril, mid) + _dot(tril, low)
            a_cs_t = a_cs.T
            ctx["chunk"] = dict(
                dt_t=dt_t, a_cs=a_cs, a_cs_t=a_cs_t,
                w_end_t=dt_t * jnp.exp(a_cs_t[:, CHUNK - 1:CHUNK] - a_cs_t),
                cdec=jnp.exp(a_cs[CHUNK - 1:CHUNK, :]))
        return unit

    def group_unit(c, g):
        def unit():
            rows = slice(c * CHUNK, (c + 1) * CHUNK)
            bm_t = xact[b0 + g, rows, :].T
            cm_g = xact[c0 + g, rows, :]
            bm_t = bm_t.astype(BF16)
            cm_g = cm_g.astype(BF16)
            ctx["group"] = (bm_t, cm_g, _dot(cm_g, bm_t).astype(BF16))
        return unit

    def pair_unit(c, sl):
        def unit():
            rows = slice(c * CHUNK, (c + 1) * CHUNK)
            ch = ctx["chunk"]
            bm_t, cm_g, cb = ctx["group"]
            xs_p = xact[sl, rows, :]
            st_p = state[:, _slab(sl)]
            y_p = None
            s_new = None
            for par in range(2):
                hd = 2 * sl + par
                keep = lo if par == 0 else jnp.logical_not(lo)
                xs_m = jnp.where(keep, xs_p, 0.0).astype(BF16)
                st_m = jnp.where(keep, st_p, 0.0).astype(BF16)
                a_l = jnp.broadcast_to(ch["a_cs"][:, hd:hd + 1], (CHUNK, CHUNK))
                a_s = ch["a_cs_t"][hd:hd + 1, :]
                decay = jnp.exp(jnp.where(causal, a_l - a_s, -jnp.inf))
                m_in = cb * decay.astype(BF16) * ch["dt_t"][hd:hd + 1, :].astype(BF16)
                c_off = cm_g * jnp.exp(a_l).astype(BF16)
                lhs = jnp.concatenate([m_in, c_off], axis=1)
                rhs = jnp.concatenate([xs_m, st_m], axis=0)
                y_h = _dot(lhs, rhs)
                b_dec_t = bm_t * ch["w_end_t"][hd:hd + 1, :].astype(BF16)
                s_h = _dot(b_dec_t, xs_m)
                y_p = y_h if y_p is None else y_p + y_h
                s_new = s_h if s_new is None else s_new + s_h
            cd_p = jnp.where(lo, ch["cdec"][:, 2 * sl:2 * sl + 1], ch["cdec"][:, 2 * sl + 1:2 * sl + 2])
            state[:, _slab(sl)] = cd_p * st_p + s_new
            yscr[rows, _slab(sl)] = y_p + xs_p * dskip_ref[:, _slab(sl)]
        return unit

    def z_unit(j):
        def unit():
            cols = slice(2 * j * LANES, (2 * j + 2) * LANES)
            zbuf[:, cols] = _dot(h, win_ref[:, cols])
        return unit

    pairs_per_group = HEADS_PER_GROUP // 2
    scan_units = []
    for c in range(tm // CHUNK):
        scan_units.append(chunk_unit(c))
        for g in range(SSM_GROUPS):
            scan_units.append(group_unit(c, g))
            scan_units.extend(pair_unit(c, g * pairs_per_group + jp) for jp in range(pairs_per_group))
    for unit in _spread(scan_units, [z_unit(j) for j in range(D_INNER // (2 * LANES))]):
        unit()

    y = yscr[...] * _silu(zbuf[...])
    gw = D_INNER // SSM_GROUPS
    parts = []
    for g in range(SSM_GROUPS):
        yg = y[:, g * gw:(g + 1) * gw]
        ms = jnp.mean(yg * yg, axis=-1, keepdims=True)
        parts.append((yg * lax.rsqrt(ms + NORM_EPS) * ng_ref[:, g * gw:(g + 1) * gw]).astype(BF16))
    yn = jnp.concatenate(parts, axis=1)
    o_ref[...] = x + gate * _dot(yn, wo_ref[...])


def _ssd(x, mod, norm_g, w_in, w_dt, conv_w, conv_b, dt_bias, a_log, d_skip, norm_y, w_out):
    s = x.shape[0]
    tm = TM_SSD
    nw = SSD_WCHUNKS
    assert s % tm == 0 and tm % CHUNK == 0 and D_MODEL % (nw * 16) == 0 and D_INNER % (nw * 16) == 0

    def tile(i):
        return (jnp.maximum(i - nw, 0), 0)

    def wchunk(i):
        return (0, jnp.minimum(i, nw - 1), 0)

    return pl.pallas_call(
        functools.partial(_ssd_kernel, tm=tm),
        grid=(nw + s // tm,),
        in_specs=[
            pl.BlockSpec((tm, D_MODEL), tile),
            _const_spec(mod.shape),
            _const_spec(norm_g.shape),
            pl.BlockSpec((1, D_MODEL // nw, w_in.shape[2]), wchunk),
            _const_spec(w_dt.shape),
            _const_spec(conv_w.shape),
            _const_spec(conv_b.shape),
            _const_spec(dt_bias.shape),
            _const_spec(a_log.shape),
            _const_spec(d_skip.shape),
            _const_spec(norm_y.shape),
            pl.BlockSpec((1, D_INNER // nw, D_MODEL), wchunk),
        ],
        out_specs=pl.BlockSpec((tm, D_MODEL), tile),
        out_shape=jax.ShapeDtypeStruct((s, D_MODEL), F32),
        scratch_shapes=[
            pltpu.VMEM((XBC_WIDTH // LANES, CONV_HALO + tm, LANES), F32),
            pltpu.VMEM((XBC_WIDTH // LANES, tm, LANES), F32),
            pltpu.VMEM((tm, LANES), F32),
            pltpu.VMEM((tm, LANES), F32),
            pltpu.VMEM((D_STATE, D_INNER), F32),
            pltpu.VMEM((tm, D_INNER), F32),
            pltpu.VMEM((tm, D_INNER), F32),
            pltpu.VMEM((D_MODEL, w_in.shape[2]), BF16),
            pltpu.VMEM((D_INNER, D_MODEL), BF16),
        ],
        compiler_params=pltpu.CompilerParams(
            dimension_semantics=("arbitrary",), vmem_limit_bytes=VMEM_LIMIT_BYTES),
        name="ssd",
    )(x, mod, norm_g, w_in, w_dt, conv_w, conv_b, dt_bias, a_log, d_skip, norm_y, w_out)


def _pad_lanes(v, width=LANES):
    return jnp.pad(v, ((0, 0), (0, width - v.shape[1])))


def kernel(x, c, positions, w_mod, b_mod, norm_mix, norm_ffn, w_in_e, b_in_e, attn_sinks, conf_conv_w,
           conf_conv_b, conf_ln_g, conf_ln_b, w_out_e, w_in_o, ssm_conv_w, ssm_conv_b, ssm_dt_bias, ssm_a_log,
           ssm_d, ssm_norm_g, w_out_o, ffn_w_gate, ffn_conv_w, ffn_conv_b, ffn_w_val, ffn_w_down, final_norm):
    b, s, d = x.shape
    assert b == 1 and d == D_MODEL and w_mod.shape[0] == 2
    xs = x.reshape(s, d)
    pos = positions.reshape(1, s)

    mod = _modulation(c, w_mod, b_mod)

    half = ROT_DIM // 2
    inv_freq = 1.0 / (ROPE_THETA ** (jnp.arange(half, dtype=F32) * 2.0 / ROT_DIM))
    freq = jnp.concatenate([inv_freq, inv_freq]).reshape(ROT_DIM, 1)
    hl = np.arange(LANES) % HEAD_DIM
    fr = np.arange(ROT_DIM)[:, None]
    is_cos = fr < half
    place = np.concatenate([
        np.where(is_cos & (hl[None, :] < ROT_DIM) & (hl[None, :] % half == fr), 1.0, 0.0),
        np.where(~is_cos & (hl[None, :] >= half) & (hl[None, :] < ROT_DIM) & (hl[None, :] % half == fr - half), 1.0, 0.0),
        np.where(~is_cos & (hl[None, :] < half) & (hl[None, :] == fr - half), -1.0, 0.0)], axis=1)
    place = jnp.asarray(place, BF16)

    row = lambda v: v.reshape(1, -1)
    final_g = row(final_norm)

    xs = _mix0(xs, pos, mod[0], row(norm_mix[0]), freq, place, w_in_e[0].astype(BF16), row(b_in_e[0]), attn_sinks[0],
               jnp.pad(conf_conv_w[0], ((0, 1), (0, 0))), row(conf_conv_b[0]), row(conf_ln_g[0]),
               row(conf_ln_b[0]), w_out_e[0].astype(BF16))
    w_gate, w_val, w_down = ffn_w_gate, ffn_w_val, ffn_w_down
    xs = _ffn(xs, mod[0], row(norm_ffn[0]), w_gate, ffn_conv_w[0], row(ffn_conv_b[0]), w_val, w_down, final_g, 0, False)

    w_dt = _pad_lanes(w_in_o[0][:, D_INNER + XBC_WIDTH:]).astype(BF16)
    d_skip = jnp.repeat(ssm_d[0], SSM_HEADDIM).reshape(1, D_INNER)
    xs = _ssd(xs, mod[1], row(norm_mix[1]), w_in_o, w_dt, ssm_conv_w[0], row(ssm_conv_b[0]),
              _pad_lanes(row(ssm_dt_bias[0])), _pad_lanes(row(ssm_a_log[0])), d_skip, row(ssm_norm_g[0]),
              w_out_o)
    xs = _ffn(xs, mod[1], row(norm_ffn[1]), w_gate, ffn_conv_w[1], row(ffn_conv_b[1]), w_val, w_down, final_g, 1, True)
    return xs.reshape(b, s, d)
```
